```python
import math
import jax, jax.numpy as jnp
from jax import lax
import numpy as np


D_MODEL = 2048
BATCH = 2
SEQ = 4096
DEPTH = 2

EPS = 1e-6
GRID_W = 64
HEAD_DIM = 128
MIX_WIDTH = D_MODEL
N_MIX_HEADS = MIX_WIDTH // HEAD_DIM
NA_HEADS = N_MIX_HEADS // 4
NA_KH = 8
NA_KW = 16
GQA_HEADS = N_MIX_HEADS // 2
GQA_KV = GQA_HEADS // 4
GQA_GROUP = GQA_HEADS // GQA_KV
AXIAL_DIM = HEAD_DIM // 2
DIFF_HEADS = N_MIX_HEADS // 4
DIFF_QK_DIM = HEAD_DIM // 2
DIFF_V_DIM = HEAD_DIM
A_COLS = 3 * NA_HEADS * HEAD_DIM
B_COLS = (GQA_HEADS + 2 * GQA_KV) * HEAD_DIM
C_COLS = DIFF_HEADS * (2 * 2 * DIFF_QK_DIM + DIFF_V_DIM)
IN_COLS = A_COLS + B_COLS + C_COLS
OUT_COLS = NA_HEADS * HEAD_DIM + GQA_HEADS * HEAD_DIM + DIFF_HEADS * DIFF_V_DIM
D_FF = 5632
PLE_DIM = 256
Q_BLOCK = 128
ROPE_THETA = 10000.0

kernel_name = "hybrid_na_gqa_diff_macaron_encoder"


def _rmsnorm(x, g):
    xf = x.astype(jnp.float32)
    y = xf * lax.rsqrt(jnp.mean(xf * xf, axis=-1, keepdims=True) + EPS)
    return (y * g.astype(jnp.float32)).astype(x.dtype)


def _swiglu(x, wg, wu, wd):
    return (jax.nn.silu(x @ wg) * (x @ wu)) @ wd


def _rope(x, cos, sin):
    half = x.shape[-1] // 2
    x1, x2 = x[..., :half], x[..., half:]
    return jnp.concatenate([x1 * cos - x2 * sin, x2 * cos + x1 * sin], axis=-1)


def _neighbourhood_attention(q, k, v, rpb, rows):
    B, S, H, d = q.shape
    kh = min(NA_KH, rows)
    kw = NA_KW
    qg = (q * (d ** -0.5)).reshape(B, rows, GRID_W, H, d)
    kg = k.reshape(B, rows, GRID_W, H, d)
    vg = v.reshape(B, rows, GRID_W, H, d)
    row_start = jnp.clip(jnp.arange(rows) - kh // 2, 0, rows - kh)
    col_idx = jnp.clip(jnp.arange(GRID_W) - kw // 2, 0, GRID_W - kw)[:, None] + jnp.arange(kw)[None, :]
    dc = col_idx - jnp.arange(GRID_W)[:, None] + (NA_KW - 1)

    def one_row(r):
        rs = row_start[r]
        q_r = lax.dynamic_index_in_dim(qg, r, axis=1, keepdims=False)
        k_win = lax.dynamic_slice_in_dim(kg, rs, kh, axis=1)[:, :, col_idx]
        v_win = lax.dynamic_slice_in_dim(vg, rs, kh, axis=1)[:, :, col_idx]
        dr = rs + jnp.arange(kh) - r + (NA_KH - 1)
        bias = rpb[:, dr[None, :, None], dc[:, None, :]]
        s = jnp.einsum('bchd,bicjhd->bhcij', q_r, k_win).astype(jnp.float32) + bias.astype(jnp.float32)
        pr = jax.nn.softmax(s.reshape(B, H, GRID_W, kh * kw), axis=-1).reshape(B, H, GRID_W, kh, kw)
        return jnp.einsum('bhcij,bicjhd->bchd', pr.astype(v.dtype), v_win)

    out = lax.map(one_row, jnp.arange(rows))
    return jnp.moveaxis(out, 0, 1).reshape(B, S, H * d)


def _gqa_axial_attention(q, k, v, q_gain, k_gain, cos, sin):
    B, S, _, d = q.shape
    nb = S // Q_BLOCK
    q = _rope(_rmsnorm(q, q_gain), cos, sin)
    k = _rope(_rmsnorm(k, k_gain), cos, sin)
    qh = q.reshape(B, S, GQA_KV, GQA_GROUP, d).transpose(0, 2, 3, 1, 4)
    kh = k.transpose(0, 2, 1, 3)
    vh = v.transpose(0, 2, 1, 3)
    qb = jnp.moveaxis(qh.reshape(B, GQA_KV, GQA_GROUP, nb, Q_BLOCK, d), 3, 0)
    scale = d ** -0.5

    def blk(qi):
        s = jnp.einsum('bkgqd,bksd->bkgqs', qi, kh).astype(jnp.float32) * scale
        pr = jax.nn.softmax(s, axis=-1)
        return jnp.einsum('bkgqs,bksd->bkgqd', pr.astype(v.dtype), vh)

    o = lax.map(blk, qb)
    o = jnp.moveaxis(o, 0, 3).reshape(B, GQA_KV, GQA_GROUP, S, d)
    return o.transpose(0, 3, 1, 2, 4).reshape(B, S, GQA_HEADS * d)


def _diff_attention(q, k, v, lam_params, subln_gain, cos, sin, layer):
    B, S, _ = q.shape
    nb = S // Q_BLOCK
    q = _rope(q.reshape(B, S, DIFF_HEADS, 2, DIFF_QK_DIM), cos, sin)
    k = _rope(k.reshape(B, S, DIFF_HEADS, 2, DIFF_QK_DIM), cos, sin)
    v = v.reshape(B, S, DIFF_HEADS, DIFF_V_DIM)
    lam_init = 0.8 - 0.6 * math.exp(-0.3 * layer)
    lp = lam_params.astype(jnp.float32)
    lam = jnp.exp(jnp.sum(lp[0] * lp[1])) - jnp.exp(jnp.sum(lp[2] * lp[3])) + lam_init
    qh = q.transpose(0, 2, 3, 1, 4)
    kh = k.transpose(0, 2, 3, 1, 4)
    vh = v.transpose(0, 2, 1, 3)
    qb = jnp.moveaxis(qh.reshape(B, DIFF_HEADS, 2, nb, Q_BLOCK, DIFF_QK_DIM), 3, 0)
    scale = DIFF_QK_DIM ** -0.5

    def blk(qi):
        s = jnp.einsum('bhcqd,bhcsd->bhcqs', qi, kh).astype(jnp.float32) * scale
        pr = jax.nn.softmax(s, axis=-1)
        a = pr[:, :, 0] - lam * pr[:, :, 1]
        return jnp.einsum('bhqs,bhsd->bhqd', a.astype(v.dtype), vh)

    o = lax.map(blk, qb)
    o = jnp.moveaxis(o, 0, 2).reshape(B, DIFF_HEADS, S, DIFF_V_DIM)
    o = _rmsnorm(o, subln_gain) * (1.0 - lam_init)
    return o.transpose(0, 2, 1, 3).reshape(B, S, DIFF_HEADS * DIFF_V_DIM)


def setup_inputs(seed: int = 0) -> dict:
    key = jax.random.key(seed)
    ks = jax.random.split(key, 24)
    f32 = jnp.float32

    def nrm(k, shape, scale):
        return jax.random.normal(k, shape, f32) * scale

    def gain(k, shape):
        return 1.0 + 0.01 * jax.random.normal(k, shape, f32)

    return {
        "x": nrm(ks[0], (BATCH, SEQ, D_MODEL), 1.0),
        "p": nrm(ks[1], (DEPTH, BATCH, SEQ, PLE_DIM), 1.0),
        "g_ffn1": gain(ks[2], (DEPTH, D_MODEL)),
        "w1_gate": nrm(ks[3], (DEPTH, D_MODEL, D_FF), D_MODEL ** -0.5),
        "w1_up": nrm(ks[4], (DEPTH, D_MODEL, D_FF), D_MODEL ** -0.5),
        "w1_down": nrm(ks[5], (DEPTH, D_FF, D_MODEL), D_FF ** -0.5),
        "g_mix": gain(ks[6], (DEPTH, D_MODEL)),
        "w_in": nrm(ks[7], (DEPTH, D_MODEL, IN_COLS), D_MODEL ** -0.5),
        "na_rpb": nrm(ks[8], (DEPTH, NA_HEADS, 2 * NA_KH - 1, 2 * NA_KW - 1), 0.1),
        "gqa_q_gain": gain(ks[9], (DEPTH, HEAD_DIM)),
        "gqa_k_gain": gain(ks[10], (DEPTH, HEAD_DIM)),
        "diff_lambda": nrm(ks[11], (DEPTH, 4, DIFF_QK_DIM), 0.1),
        "diff_subln_gain": gain(ks[12], (DEPTH, DIFF_V_DIM)),
        "w_out": nrm(ks[13], (DEPTH, OUT_COLS, D_MODEL), OUT_COLS ** -0.5),
        "g_ffn2": gain(ks[14], (DEPTH, D_MODEL)),
        "w2_gate": nrm(ks[15], (DEPTH, D_MODEL, D_FF), D_MODEL ** -0.5),
        "w2_up": nrm(ks[16], (DEPTH, D_MODEL, D_FF), D_MODEL ** -0.5),
        "w2_down": nrm(ks[17], (DEPTH, D_FF, D_MODEL), D_FF ** -0.5),
        "g_ple": gain(ks[18], (DEPTH, D_MODEL)),
        "w_ple_gate": nrm(ks[19], (DEPTH, D_MODEL, D_MODEL), D_MODEL ** -0.5),
        "w_ple_proj": nrm(ks[20], (DEPTH, PLE_DIM, D_MODEL), PLE_DIM ** -0.5),
        "g_final": gain(ks[21], (D_MODEL,)),
    }


def reference(x, p, g_ffn1, w1_gate, w1_up, w1_down, g_mix, w_in, na_rpb, gqa_q_gain, gqa_k_gain,
              diff_lambda, diff_subln_gain, w_out, g_ffn2, w2_gate, w2_up, w2_down, g_ple,
              w_ple_gate, w_ple_proj, g_final):
    B, S, _ = x.shape
    rows = S // GRID_W
    f32 = jnp.float32
    t = jnp.arange(S)
    inv_ax = jnp.power(ROPE_THETA, -jnp.arange(0, AXIAL_DIM, 2, dtype=f32) / AXIAL_DIM)
    ang_b = jnp.concatenate([(t // GRID_W).astype(f32)[:, None] * inv_ax,
                             (t % GRID_W).astype(f32)[:, None] * inv_ax], axis=-1)
    cos_b = jnp.cos(ang_b)[:, None, :].astype(x.dtype)
    sin_b = jnp.sin(ang_b)[:, None, :].astype(x.dtype)
    inv_1d = jnp.power(ROPE_THETA, -jnp.arange(0, DIFF_QK_DIM, 2, dtype=f32) / DIFF_QK_DIM)
    ang_c = t.astype(f32)[:, None] * inv_1d
    cos_c = jnp.cos(ang_c)[:, None, None, :].astype(x.dtype)
    sin_c = jnp.sin(ang_c)[:, None, None, :].astype(x.dtype)

    h = x
    for i in range(DEPTH):
        h = h + 0.5 * _swiglu(_rmsnorm(h, g_ffn1[i]), w1_gate[i], w1_up[i], w1_down[i])
        proj = _rmsnorm(h, g_mix[i]) @ w_in[i]
        a_part, b_part, c_part = jnp.split(proj, [A_COLS, A_COLS + B_COLS], axis=-1)
        qa, ka, va = [u.reshape(B, S, NA_HEADS, HEAD_DIM) for u in jnp.split(a_part, 3, axis=-1)]
        qb_, kb_, vb_ = jnp.split(b_part, [GQA_HEADS * HEAD_DIM, (GQA_HEADS + GQA_KV) * HEAD_DIM], axis=-1)
        qb_ = qb_.reshape(B, S, GQA_HEADS, HEAD_DIM)
        kb_ = kb_.reshape(B, S, GQA_KV, HEAD_DIM)
        vb_ = vb_.reshape(B, S, GQA_KV, HEAD_DIM)
        qc, kc, vc = jnp.split(c_part, [DIFF_HEADS * 2 * DIFF_QK_DIM, DIFF_HEADS * 4 * DIFF_QK_DIM], axis=-1)
        out_a = _neighbourhood_attention(qa, ka, va, na_rpb[i], rows)
        out_b = _gqa_axial_attention(qb_, kb_, vb_, gqa_q_gain[i], gqa_k_gain[i], cos_b, sin_b)
        out_c = _diff_attention(qc, kc, vc, diff_lambda[i], diff_subln_gain[i], cos_c, sin_c, i)
        h = h + jnp.concatenate([out_a, out_b, out_c], axis=-1) @ w_out[i]
        h = h + 0.5 * _swiglu(_rmsnorm(h, g_ffn2[i]), w2_gate[i], w2_up[i], w2_down[i])
        gate = jax.nn.sigmoid(_rmsnorm(h, g_ple[i]) @ w_ple_gate[i])
        h = h + gate * (p[i] @ w_ple_proj[i])
    return _rmsnorm(h, g_final)
```

```python
import functools
import math

import numpy as np
import jax
import jax.numpy as jnp
from jax import lax
from jax.experimental import pallas as pl
from jax.experimental.pallas import tpu as pltpu

F32 = jnp.float32
BF16 = jnp.bfloat16

D_MODEL = 2048
EPS = 1e-6
GRID_W = 64
HEAD_DIM = 128
NA_HEADS = 4
NA_KH = 8
NA_KW = 16
GQA_HEADS = 8
GQA_KV = 2
GQA_GROUP = GQA_HEADS // GQA_KV
AXIAL_DIM = HEAD_DIM // 2
DIFF_HEADS = 4
DIFF_QK_DIM = HEAD_DIM // 2
IN_COLS = 4608
D_FF = 5632
PLE_DIM = 256
ROPE_THETA = 10000.0

NEG_BIG = -1e30
VMEM_LIMIT_BYTES = 56 * 1024 * 1024

FFN_TM = 512
FFN_TF = 512
PROJ_TM = 1024
PROJ_TN = 512
OUT_TM = 1024
OUT_TN = 1024
PLE_TM = 512
ATT_TQ = 512
ATT_RB = 256
NA_QROWS = 8
NA_KROWS = 16


def _params(*sem):
    return pltpu.CompilerParams(dimension_semantics=sem, vmem_limit_bytes=VMEM_LIMIT_BYTES)


def _rms_rows(x, g):
    return x * lax.rsqrt(jnp.mean(x * x, axis=-1, keepdims=True) + EPS) * g


def _ffn_kernel(h_ref, g_ref, wg_ref, wu_ref, wd_ref, o_ref, xn_ref):
    j = pl.program_id(1)

    @pl.when(j == 0)
    def _():
        xn_ref[...] = _rms_rows(h_ref[...], g_ref[...]).astype(BF16)

    xn = xn_ref[...]
    a = jnp.dot(xn, wg_ref[...], preferred_element_type=F32)
    u = jnp.dot(xn, wu_ref[...], preferred_element_type=F32)
    act = (a * jax.nn.sigmoid(a) * (0.5 * u)).astype(BF16)
    contrib = jnp.dot(act, wd_ref[...], preferred_element_type=F32)

    @pl.when(j == 0)
    def _():
        o_ref[...] = h_ref[...] + contrib

    @pl.when(j > 0)
    def _():
        o_ref[...] += contrib


def _ffn(h, g, wg, wu, wd):
    T = h.shape[0]
    return pl.pallas_call(
        _ffn_kernel,
        grid=(T // FFN_TM, D_FF // FFN_TF),
        in_specs=[
            pl.BlockSpec((FFN_TM, D_MODEL), lambda i, j: (i, 0)),
            pl.BlockSpec((1, D_MODEL), lambda i, j: (0, 0)),
            pl.BlockSpec((D_MODEL, FFN_TF), lambda i, j: (0, j)),
            pl.BlockSpec((D_MODEL, FFN_TF), lambda i, j: (0, j)),
            pl.BlockSpec((FFN_TF, D_MODEL), lambda i, j: (j, 0)),
        ],
        out_specs=pl.BlockSpec((FFN_TM, D_MODEL), lambda i, j: (i, 0)),
        out_shape=jax.ShapeDtypeStruct((T, D_MODEL), F32),
        scratch_shapes=[pltpu.VMEM((FFN_TM, D_MODEL), BF16)],
        compiler_params=_params("parallel", "arbitrary"),
        name="ffn",
    )(h, g.reshape(1, D_MODEL), wg, wu, wd)


def _proj_kernel(h_ref, g_ref, w_ref, qg_ref, kg_ref, cb_ref, sb_ref, cc_ref, sca_ref, scb_ref,
                 o_ref, xn_ref):
    j = pl.program_id(1)

    @pl.when(j == 0)
    def _():
        xn_ref[...] = _rms_rows(h_ref[...], g_ref[...]).astype(BF16)

    y = jnp.dot(xn_ref[...], w_ref[...], preferred_element_type=F32)
    heads = [y[:, k * HEAD_DIM:(k + 1) * HEAD_DIM] for k in range(PROJ_TN // HEAD_DIM)]

    def put(k, val):
        o_ref[:, k * HEAD_DIM:(k + 1) * HEAD_DIM] = val.astype(BF16)

    def rope_axial(x):
        return x * cb_ref[...] + pltpu.roll(x, HEAD_DIM // 2, 1) * sb_ref[...]

    def rope_pair(x):
        return (x * cc_ref[...] + pltpu.roll(x, HEAD_DIM - DIFF_QK_DIM // 2, 1) * sca_ref[...]
                + pltpu.roll(x, DIFF_QK_DIM // 2, 1) * scb_ref[...])

    @pl.when(j == 0)
    def _():
        for k in range(4):
            put(k, heads[k] * (HEAD_DIM ** -0.5))

    @pl.when((j == 1) | (j == 2) | (j == 8))
    def _():
        for k in range(4):
            put(k, heads[k])

    @pl.when((j == 3) | (j == 4))
    def _():
        for k in range(4):
            put(k, rope_axial(_rms_rows(heads[k], qg_ref[...])) * (HEAD_DIM ** -0.5))

    @pl.when(j == 5)
    def _():
        for k in range(2):
            put(k, rope_axial(_rms_rows(heads[k], kg_ref[...])))
        for k in range(2, 4):
            put(k, heads[k])

    @pl.when(j == 6)
    def _():
        for k in range(4):
            put(k, rope_pair(heads[k]) * (DIFF_QK_DIM ** -0.5))

    @pl.when(j == 7)
    def _():
        for k in range(4):
            put(k, rope_pair(heads[k]))


def _proj(h, g, w, qg, kg, tables, S):
    T = h.shape[0]
    nst = S // PROJ_TM
    tab_spec = pl.BlockSpec((PROJ_TM, HEAD_DIM), lambda i, j: (i % nst, 0))
    vec_spec = pl.BlockSpec((1, HEAD_DIM), lambda i, j: (0, 0))
    return pl.pallas_call(
        _proj_kernel,
        grid=(T // PROJ_TM, IN_COLS // PROJ_TN),
        in_specs=[
            pl.BlockSpec((PROJ_TM, D_MODEL), lambda i, j: (i, 0)),
            pl.BlockSpec((1, D_MODEL), lambda i, j: (0, 0)),
            pl.BlockSpec((D_MODEL, PROJ_TN), lambda i, j: (0, j)),
            vec_spec, vec_spec, tab_spec, tab_spec, tab_spec, tab_spec, tab_spec,
        ],
        out_specs=pl.BlockSpec((PROJ_TM, PROJ_TN), lambda i, j: (i, j)),
        out_shape=jax.ShapeDtypeStruct((T, IN_COLS), BF16),
        scratch_shapes=[pltpu.VMEM((PROJ_TM, D_MODEL), BF16)],
        compiler_params=_params("parallel", "arbitrary"),
        name="proj",
    )(h, g.reshape(1, D_MODEL), w, qg.reshape(1, HEAD_DIM), kg.reshape(1, HEAD_DIM), *tables)


def _transpose_keys(k_ref, kt_ref):
    S = k_ref.shape[0]
    step = 512
    for c in range(S // step):
        blk = k_ref[c * step:(c + 1) * step, :].astype(F32)
        kt_ref[:, c * step:(c + 1) * step] = blk.T.astype(BF16)


def _softmax_pv(q, kt_ref, v_ref):
    s = jnp.dot(q, kt_ref[...], preferred_element_type=F32)
    m = jnp.max(s, axis=-1, keepdims=True)
    p = jnp.exp(s - m)
    l = jnp.sum(p, axis=-1, keepdims=True)
    o = jnp.dot(p.astype(BF16), v_ref[...], preferred_element_type=F32)
    return o / l


def _gqa_kernel(q_ref, k_ref, v_ref, o_ref, kt_ref):
    @pl.when((pl.program_id(2) == 0) & (pl.program_id(3) == 0))
    def _():
        _transpose_keys(k_ref, kt_ref)

    def body(r, carry):
        off = pl.multiple_of(r * ATT_RB, ATT_RB)
        o = _softmax_pv(q_ref[pl.ds(off, ATT_RB), :], kt_ref, v_ref)
        o_ref[pl.ds(off, ATT_RB), :] = o.astype(o_ref.dtype)
        return carry

    lax.fori_loop(0, ATT_TQ // ATT_RB, body, 0)


def _gqa(proj, B, S):
    T = proj.shape[0]
    nq = S // ATT_TQ
    q0 = 1536 // HEAD_DIM
    k0 = (1536 + GQA_HEADS * HEAD_DIM) // HEAD_DIM
    v0 = k0 + GQA_KV
    return pl.pallas_call(
        _gqa_kernel,
        grid=(B, GQA_KV, GQA_GROUP, nq),
        in_specs=[
            pl.BlockSpec((ATT_TQ, HEAD_DIM), lambda b, kh, g, i: (b * nq + i, q0 + kh * GQA_GROUP + g)),
            pl.BlockSpec((S, HEAD_DIM), lambda b, kh, g, i: (b, k0 + kh)),
            pl.BlockSpec((S, HEAD_DIM), lambda b, kh, g, i: (b, v0 + kh)),
        ],
        out_specs=pl.BlockSpec((ATT_TQ, HEAD_DIM), lambda b, kh, g, i: (b * nq + i, kh * GQA_GROUP + g)),
        out_shape=jax.ShapeDtypeStruct((T, GQA_HEADS * HEAD_DIM), BF16),
        scratch_shapes=[pltpu.VMEM((HEAD_DIM, S), BF16)],
        compiler_params=_params("parallel", "parallel", "arbitrary", "arbitrary"),
        name="gqa",
    )(proj, proj, proj)


def _diff_kernel(lam_ref, gain_ref, q_ref, k_ref, v_ref, o_ref, kt_ref, *, lam_init):
    @pl.when(pl.program_id(2) == 0)
    def _():
        _transpose_keys(k_ref, kt_ref)

    lp = lam_ref[...]
    lam = (jnp.exp(jnp.sum(lp[0:1] * lp[1:2], axis=-1, keepdims=True))
           - jnp.exp(jnp.sum(lp[2:3] * lp[3:4], axis=-1, keepdims=True)) + lam_init)
    lane = lax.broadcasted_iota(jnp.int32, (ATT_RB, HEAD_DIM), 1)
    first = lane < DIFF_QK_DIM

    def body(r, carry):
        off = pl.multiple_of(r * ATT_RB, ATT_RB)
        q = q_ref[pl.ds(off, ATT_RB), :]
        zero = jnp.zeros_like(q)
        o1 = _softmax_pv(jnp.where(first, q, zero), kt_ref, v_ref)
        o2 = _softmax_pv(jnp.where(first, zero, q), kt_ref, v_ref)
        o = o1 - lam * o2
        o = _rms_rows(o, gain_ref[...]) * (1.0 - lam_init)
        o_ref[pl.ds(off, ATT_RB), :] = o.astype(o_ref.dtype)
        return carry

    lax.fori_loop(0, ATT_TQ // ATT_RB, body, 0)


def _diff(proj, lam_params, gain, layer, B, S):
    T = proj.shape[0]
    nq = S // ATT_TQ
    q0 = 3072 // HEAD_DIM
    k0 = q0 + DIFF_HEADS
    v0 = k0 + DIFF_HEADS
    lam_init = 0.8 - 0.6 * math.exp(-0.3 * layer)
    return pl.pallas_call(
        functools.partial(_diff_kernel, lam_init=lam_init),
        grid=(B, DIFF_HEADS, nq),
        in_specs=[
            pl.BlockSpec((4, DIFF_QK_DIM), lambda b, h, i: (0, 0)),
            pl.BlockSpec((1, HEAD_DIM), lambda b, h, i: (0, 0)),
            pl.BlockSpec((ATT_TQ, HEAD_DIM), lambda b, h, i: (b * nq + i, q0 + h)),
            pl.BlockSpec((S, HEAD_DIM), lambda b, h, i: (b, k0 + h)),
            pl.BlockSpec((S, HEAD_DIM), lambda b, h, i: (b, v0 + h)),
        ],
        out_specs=pl.BlockSpec((ATT_TQ, HEAD_DIM), lambda b, h, i: (b * nq + i, h)),
        out_shape=jax.ShapeDtypeStruct((T, DIFF_HEADS * HEAD_DIM), BF16),
        scratch_shapes=[pltpu.VMEM((HEAD_DIM, S), BF16)],
        compiler_params=_params("parallel", "parallel", "arbitrary"),
        name="diff",
    )(lam_params, gain.reshape(1, HEAD_DIM), proj, proj, proj)


def _na_kernel(q_ref, k_ref, v_ref, bias_ref, o_ref, *, rows):
    blk = pl.program_id(2)
    ws = jnp.clip(blk * NA_QROWS - NA_KH // 2, 0, rows - NA_KROWS) * GRID_W
    ws = pl.multiple_of(ws, (NA_KH // 2) * GRID_W)
    k = k_ref[pl.ds(ws, NA_KROWS * GRID_W), :]
    v = v_ref[pl.ds(ws, NA_KROWS * GRID_W), :]
    s = lax.dot_general(q_ref[...], k, (((1,), (1,)), ((), ())), preferred_element_type=F32)
    s = s + bias_ref[0, 0]
    m = jnp.max(s, axis=-1, keepdims=True)
    p = jnp.exp(s - m)
    l = jnp.sum(p, axis=-1, keepdims=True)
    o = jnp.dot(p.astype(BF16), v, preferred_element_type=F32)
    o_ref[...] = (o / l).astype(o_ref.dtype)


def _na_block_type(blk, nblk):
    return jnp.minimum(blk, 1) + blk // (nblk - 1)


def _na(proj, bias, B, S):
    T = proj.shape[0]
    rows = S // GRID_W
    nblk = rows // NA_QROWS
    tq = NA_QROWS * GRID_W
    tk = NA_KROWS * GRID_W
    return pl.pallas_call(
        functools.partial(_na_kernel, rows=rows),
        grid=(B, NA_HEADS, nblk),
        in_specs=[
            pl.BlockSpec((tq, HEAD_DIM), lambda b, h, i: (b * nblk + i, h)),
            pl.BlockSpec((S, HEAD_DIM), lambda b, h, i: (b, NA_HEADS + h)),
            pl.BlockSpec((S, HEAD_DIM), lambda b, h, i: (b, 2 * NA_HEADS + h)),
            pl.BlockSpec((1, 1, tq, tk), lambda b, h, i: (h, _na_block_type(i, nblk), 0, 0)),
        ],
        out_specs=pl.BlockSpec((tq, HEAD_DIM), lambda b, h, i: (b * nblk + i, h)),
        out_shape=jax.ShapeDtypeStruct((T, NA_HEADS * HEAD_DIM), BF16),
        compiler_params=_params("parallel", "parallel", "arbitrary"),
        name="na",
    )(proj, proj, proj, bias)


def _na_bias_tables(rpb, rows):
    H = rpb.shape[0]
    cq = np.arange(GRID_W)
    cs = np.clip(cq - NA_KW // 2, 0, GRID_W - NA_KW)
    ck = np.arange(GRID_W)
    valid_c = (ck[None, :] >= cs[:, None]) & (ck[None, :] < cs[:, None] + NA_KW)
    dc = np.clip(ck[None, :] - cq[:, None] + (NA_KW - 1), 0, 2 * NA_KW - 2)
    cexp = jnp.where(jnp.asarray(valid_c)[None, None], rpb[:, :, dc], NEG_BIG)
    cexp = jnp.concatenate([cexp, jnp.full((H, 1, GRID_W, GRID_W), NEG_BIG, F32)], axis=1)
    masked_slab = 2 * NA_KH - 1
    idx = np.zeros((3, NA_QROWS, NA_KROWS), np.int32)
    for t, r0 in enumerate((0, NA_QROWS, rows - NA_QROWS)):
        ws = int(np.clip(r0 - NA_KH // 2, 0, rows - NA_KROWS))
        r = r0 + np.arange(NA_QROWS)
        rs = np.clip(r - NA_KH // 2, 0, rows - NA_KH)
        kr = ws + np.arange(NA_KROWS)
        valid_r = (kr[None, :] >= rs[:, None]) & (kr[None, :] < rs[:, None] + NA_KH)
        dr = kr[None, :] - r[:, None] + (NA_KH - 1)
        idx[t] = np.where(valid_r, dr, masked_slab)
    full = cexp[:, idx]
    full = jnp.transpose(full, (0, 1, 2, 4, 3, 5))
    return full.reshape(H, 3, NA_QROWS * GRID_W, NA_KROWS * GRID_W)


def _outproj_kernel(h_ref, a_ref, b_ref, c_ref, w_ref, o_ref):
    na = a_ref.shape[1]
    nb = b_ref.shape[1]
    acc = jnp.dot(a_ref[...], w_ref[0:na, :], preferred_element_type=F32)
    acc += jnp.dot(b_ref[...], w_ref[na:na + nb, :], preferred_element_type=F32)
    acc += jnp.dot(c_ref[...], w_ref[na + nb:, :], preferred_element_type=F32)
    o_ref[...] = h_ref[...] + acc


def _outproj(h, oa, ob, oc, w):
    T = h.shape[0]
    return pl.pallas_call(
        _outproj_kernel,
        grid=(T // OUT_TM, D_MODEL // OUT_TN),
        in_specs=[
            pl.BlockSpec((OUT_TM, OUT_TN), lambda i, j: (i, j)),
            pl.BlockSpec((OUT_TM, oa.shape[1]), lambda i, j: (i, 0)),
            pl.BlockSpec((OUT_TM, ob.shape[1]), lambda i, j: (i, 0)),
            pl.BlockSpec((OUT_TM, oc.shape[1]), lambda i, j: (i, 0)),
            pl.BlockSpec((D_MODEL, OUT_TN), lambda i, j: (0, j)),
        ],
        out_specs=pl.BlockSpec((OUT_TM, OUT_TN), lambda i, j: (i, j)),
        out_shape=jax.ShapeDtypeStruct((T, D_MODEL), F32),
        compiler_params=_params("parallel", "arbitrary"),
        name="outproj",
    )(h, oa, ob, oc, w)


def _ple_kernel(h_ref, g_ref, wg_ref, p_ref, wp_ref, gf_ref, o_ref, *, final):
    x = h_ref[...]
    xn = _rms_rows(x, g_ref[...]).astype(BF16)
    gate = jax.nn.sigmoid(jnp.dot(xn, wg_ref[...], preferred_element_type=F32))
    emb = jnp.dot(p_ref[...].astype(BF16), wp_ref[...], preferred_element_type=F32)
    y = x + gate * emb
    if final:
        y = _rms_rows(y, gf_ref[...])
    o_ref[...] = y


def _ple(h, g, wg, p, wp, g_final, final):
    T = h.shape[0]
    return pl.pallas_call(
        functools.partial(_ple_kernel, final=final),
        grid=(T // PLE_TM,),
        in_specs=[
            pl.BlockSpec((PLE_TM, D_MODEL), lambda i: (i, 0)),
            pl.BlockSpec((1, D_MODEL), lambda i: (0, 0)),
            pl.BlockSpec((D_MODEL, D_MODEL), lambda i: (0, 0)),
            pl.BlockSpec((PLE_TM, PLE_DIM), lambda i: (i, 0)),
            pl.BlockSpec((PLE_DIM, D_MODEL), lambda i: (0, 0)),
            pl.BlockSpec((1, D_MODEL), lambda i: (0, 0)),
        ],
        out_specs=pl.BlockSpec((PLE_TM, D_MODEL), lambda i: (i, 0)),
        out_shape=jax.ShapeDtypeStruct((T, D_MODEL), F32),
        compiler_params=_params("parallel"),
        name="ple",
    )(h, g.reshape(1, D_MODEL), wg, p, wp, g_final.reshape(1, D_MODEL))


def _rope_tables(S):
    t = jnp.arange(S)
    inv_ax = jnp.power(ROPE_THETA, -jnp.arange(0, AXIAL_DIM, 2, dtype=F32) / AXIAL_DIM)
    ang_b = jnp.concatenate([(t // GRID_W).astype(F32)[:, None] * inv_ax,
                             (t % GRID_W).astype(F32)[:, None] * inv_ax], axis=-1)
    cos_b, sin_b = jnp.cos(ang_b), jnp.sin(ang_b)
    cb = jnp.concatenate([cos_b, cos_b], axis=-1)
    sb = jnp.concatenate([-sin_b, sin_b], axis=-1)
    inv_1d = jnp.power(ROPE_THETA, -jnp.arange(0, DIFF_QK_DIM, 2, dtype=F32) / DIFF_QK_DIM)
    ang_c = t.astype(F32)[:, None] * inv_1d
    cos_c, sin_c = jnp.cos(ang_c), jnp.sin(ang_c)
    zero = jnp.zeros_like(sin_c)
    cc = jnp.concatenate([cos_c] * 4, axis=-1)
    sca = jnp.concatenate([-sin_c, zero, -sin_c, zero], axis=-1)
    scb = jnp.concatenate([zero, sin_c, zero, sin_c], axis=-1)
    return cb, sb, cc, sca, scb


def kernel(x, p, g_ffn1, w1_gate, w1_up, w1_down, g_mix, w_in, na_rpb, gqa_q_gain, gqa_k_gain,
           diff_lambda, diff_subln_gain, w_out, g_ffn2, w2_gate, w2_up, w2_down, g_ple,
           w_ple_gate, w_ple_proj, g_final):
    B, S, D = x.shape
    depth = w_in.shape[0]
    T = B * S
    rows = S // GRID_W
    tables = _rope_tables(S)
    h = x.reshape(T, D)
    for i in range(depth):
        h = _ffn(h, g_ffn1[i], w1_gate[i].astype(BF16), w1_up[i].astype(BF16), w1_down[i].astype(BF16))
        proj = _proj(h, g_mix[i], w_in[i].astype(BF16), gqa_q_gain[i], gqa_k_gain[i], tables, S)
        out_a = _na(proj, _na_bias_tables(na_rpb[i], rows), B, S)
        out_b = _gqa(proj, B, S)
        out_c = _diff(proj, diff_lambda[i], diff_subln_gain[i], i, B, S)
        h = _outproj(h, out_a, out_b, out_c, w_out[i].astype(BF16))
        h = _ffn(h, g_ffn2[i], w2_gate[i].astype(BF16), w2_up[i].astype(BF16), w2_down[i].astype(BF16))
        h = _ple(h, g_ple[i], w_ple_gate[i].astype(BF16), p[i].reshape(T, PLE_DIM),
                 w_ple_proj[i].astype(BF16), g_final, final=(i == depth - 1))
    return h.reshape(B, S, D)
```

```python
import functools
import math

import numpy as np
import jax
import jax.numpy as jnp
from jax import lax
from jax.experimental import pallas as pl
from jax.experimental.pallas import tpu as pltpu

F32 = jnp.float32
BF16 = jnp.bfloat16

D_MODEL = 2048
EPS = 1e-6
GRID_W = 64
HEAD_DIM = 128
NA_HEADS = 4
NA_KH = 8
NA_KW = 16
GQA_HEADS = 8
GQA_KV = 2
GQA_GROUP = GQA_HEADS // GQA_KV
AXIAL_DIM = HEAD_DIM // 2
DIFF_HEADS = 4
DIFF_QK_DIM = HEAD_DIM // 2
IN_COLS = 4608
D_FF = 5632
PLE_DIM = 256
ROPE_THETA = 10000.0

NEG_BIG = -1e30
VMEM_LIMIT_BYTES = 56 * 1024 * 1024

FFN_TM = 1024
FFN_TF = 256
PROJ_TM = 1024
PROJ_TN = 512
OUT_TM = 1024
OUT_TN = 1024
PLE_TM = 512
ATT_TQ = 512
ATT_RB = 256
NA_QROWS = 8
NA_KROWS = 16


def _params(*sem):
    return pltpu.CompilerParams(dimension_semantics=sem, vmem_limit_bytes=VMEM_LIMIT_BYTES)


def _rms_rows(x, g):
    return x * lax.rsqrt(jnp.mean(x * x, axis=-1, keepdims=True) + EPS) * g


def _ffn_kernel(h_ref, g_ref, wg_ref, wu_ref, wd_ref, o_ref, xn_ref):
    @pl.when(pl.program_id(1) == 0)
    def _():
        x = h_ref[...]
        xn_ref[...] = _rms_rows(x, g_ref[...]).astype(BF16)
        o_ref[...] = x

    xn = xn_ref[...]
    a = jnp.dot(xn, wg_ref[...].astype(BF16), preferred_element_type=F32)
    u = jnp.dot(xn, wu_ref[...].astype(BF16), preferred_element_type=F32)
    act = (a * jax.nn.sigmoid(a) * (0.5 * u)).astype(BF16)
    o_ref[...] += jnp.dot(act, wd_ref[...].astype(BF16), preferred_element_type=F32)


def _ffn(h, g, wg, wu, wd, layer):
    T = h.shape[0]
    return pl.pallas_call(
        _ffn_kernel,
        grid=(T // FFN_TM, D_FF // FFN_TF),
        in_specs=[
            pl.BlockSpec((FFN_TM, D_MODEL), lambda i, j: (i, 0), pipeline_mode=pl.Buffered(1)),
            pl.BlockSpec((1, D_MODEL), lambda i, j: (0, 0)),
            pl.BlockSpec((None, D_MODEL, FFN_TF), lambda i, j: (layer, 0, j)),
            pl.BlockSpec((None, D_MODEL, FFN_TF), lambda i, j: (layer, 0, j)),
            pl.BlockSpec((None, FFN_TF, D_MODEL), lambda i, j: (layer, j, 0)),
        ],
        out_specs=pl.BlockSpec((FFN_TM, D_MODEL), lambda i, j: (i, 0)),
        out_shape=jax.ShapeDtypeStruct((T, D_MODEL), F32),
        scratch_shapes=[pltpu.VMEM((FFN_TM, D_MODEL), BF16)],
        compiler_params=_params("parallel", "arbitrary"),
        name="ffn",
    )(h, g.reshape(1, D_MODEL), wg, wu, wd)


def _proj_kernel(h_ref, g_ref, w_ref, qg_ref, kg_ref, cb_ref, sb_ref, cc_ref, sca_ref, scb_ref,
                 o_ref, xn_ref):
    j = pl.program_id(1)

    @pl.when(j == 0)
    def _():
        xn_ref[...] = _rms_rows(h_ref[...], g_ref[...]).astype(BF16)

    y = jnp.dot(xn_ref[...], w_ref[...].astype(BF16), preferred_element_type=F32)
    heads = [y[:, k * HEAD_DIM:(k + 1) * HEAD_DIM] for k in range(PROJ_TN // HEAD_DIM)]

    def put(k, val):
        o_ref[:, k * HEAD_DIM:(k + 1) * HEAD_DIM] = val.astype(BF16)

    def rope_axial(x):
        return x * cb_ref[...] + pltpu.roll(x, HEAD_DIM // 2, 1) * sb_ref[...]

    def rope_pair(x):
        return (x * cc_ref[...] + pltpu.roll(x, HEAD_DIM - DIFF_QK_DIM // 2, 1) * sca_ref[...]
                + pltpu.roll(x, DIFF_QK_DIM // 2, 1) * scb_ref[...])

    @pl.when(j == 0)
    def _():
        for k in range(4):
            put(k, heads[k] * (HEAD_DIM ** -0.5))

    @pl.when((j == 1) | (j == 2) | (j == 8))
    def _():
        for k in range(4):
            put(k, heads[k])

    @pl.when((j == 3) | (j == 4))
    def _():
        for k in range(4):
            put(k, rope_axial(_rms_rows(heads[k], qg_ref[...])) * (HEAD_DIM ** -0.5))

    @pl.when(j == 5)
    def _():
        for k in range(2):
            put(k, rope_axial(_rms_rows(heads[k], kg_ref[...])))
        for k in range(2, 4):
            put(k, heads[k])

    @pl.when(j == 6)
    def _():
        for k in range(4):
            put(k, rope_pair(heads[k]) * (DIFF_QK_DIM ** -0.5))

    @pl.when(j == 7)
    def _():
        for k in range(4):
            put(k, rope_pair(heads[k]))


def _proj(h, g, w, qg, kg, tables, S, layer):
    T = h.shape[0]
    nst = S // PROJ_TM
    tab_spec = pl.BlockSpec((PROJ_TM, HEAD_DIM), lambda i, j: (i % nst, 0))
    vec_spec = pl.BlockSpec((1, HEAD_DIM), lambda i, j: (0, 0))
    return pl.pallas_call(
        _proj_kernel,
        grid=(T // PROJ_TM, IN_COLS // PROJ_TN),
        in_specs=[
            pl.BlockSpec((PROJ_TM, D_MODEL), lambda i, j: (i, 0)),
            pl.BlockSpec((1, D_MODEL), lambda i, j: (0, 0)),
            pl.BlockSpec((None, D_MODEL, PROJ_TN), lambda i, j: (layer, 0, j)),
            vec_spec, vec_spec, tab_spec, tab_spec, tab_spec, tab_spec, tab_spec,
        ],
        out_specs=pl.BlockSpec((PROJ_TM, PROJ_TN), lambda i, j: (i, j)),
        out_shape=jax.ShapeDtypeStruct((T, IN_COLS), BF16),
        scratch_shapes=[pltpu.VMEM((PROJ_TM, D_MODEL), BF16)],
        compiler_params=_params("parallel", "arbitrary"),
        name="proj",
    )(h, g.reshape(1, D_MODEL), w, qg.reshape(1, HEAD_DIM), kg.reshape(1, HEAD_DIM), *tables)


def _transpose_keys(k_ref, kt_ref):
    S = k_ref.shape[0]
    step = 512
    for c in range(S // step):
        blk = k_ref[c * step:(c + 1) * step, :].astype(F32)
        kt_ref[:, c * step:(c + 1) * step] = blk.T.astype(BF16)


def _softmax_pv(q, kt_ref, v_ref):
    s = jnp.dot(q, kt_ref[...], preferred_element_type=F32)
    m = jnp.max(s, axis=-1, keepdims=True)
    p = jnp.exp(s - m)
    l = jnp.sum(p, axis=-1, keepdims=True)
    o = jnp.dot(p.astype(BF16), v_ref[...], preferred_element_type=F32)
    return o / l


def _gqa_kernel(q_ref, k_ref, v_ref, o_ref, kt_ref):
    @pl.when((pl.program_id(2) == 0) & (pl.program_id(3) == 0))
    def _():
        _transpose_keys(k_ref, kt_ref)

    def body(r, carry):
        off = pl.multiple_of(r * ATT_RB, ATT_RB)
        o = _softmax_pv(q_ref[pl.ds(off, ATT_RB), :], kt_ref, v_ref)
        o_ref[pl.ds(off, ATT_RB), :] = o.astype(o_ref.dtype)
        return carry

    lax.fori_loop(0, ATT_TQ // ATT_RB, body, 0)


def _gqa(proj, B, S):
    T = proj.shape[0]
    nq = S // ATT_TQ
    q0 = 1536 // HEAD_DIM
    k0 = (1536 + GQA_HEADS * HEAD_DIM) // HEAD_DIM
    v0 = k0 + GQA_KV
    return pl.pallas_call(
        _gqa_kernel,
        grid=(B, GQA_KV, GQA_GROUP, nq),
        in_specs=[
            pl.BlockSpec((ATT_TQ, HEAD_DIM), lambda b, kh, g, i: (b * nq + i, q0 + kh * GQA_GROUP + g)),
            pl.BlockSpec((S, HEAD_DIM), lambda b, kh, g, i: (b, k0 + kh)),
            pl.BlockSpec((S, HEAD_DIM), lambda b, kh, g, i: (b, v0 + kh)),
        ],
        out_specs=pl.BlockSpec((ATT_TQ, HEAD_DIM), lambda b, kh, g, i: (b * nq + i, kh * GQA_GROUP + g)),
        out_shape=jax.ShapeDtypeStruct((T, GQA_HEADS * HEAD_DIM), BF16),
        scratch_shapes=[pltpu.VMEM((HEAD_DIM, S), BF16)],
        compiler_params=_params("parallel", "parallel", "arbitrary", "arbitrary"),
        name="gqa",
    )(proj, proj, proj)


def _diff_kernel(lam_ref, gain_ref, q_ref, k_ref, v_ref, o_ref, kt_ref, *, lam_init):
    @pl.when(pl.program_id(2) == 0)
    def _():
        _transpose_keys(k_ref, kt_ref)

    lp = lam_ref[...]
    lam = (jnp.exp(jnp.sum(lp[0:1] * lp[1:2], axis=-1, keepdims=True))
           - jnp.exp(jnp.sum(lp[2:3] * lp[3:4], axis=-1, keepdims=True)) + lam_init)
    lane = lax.broadcasted_iota(jnp.int32, (ATT_RB, HEAD_DIM), 1)
    first = lane < DIFF_QK_DIM

    def body(r, carry):
        off = pl.multiple_of(r * ATT_RB, ATT_RB)
        q = q_ref[pl.ds(off, ATT_RB), :]
        zero = jnp.zeros_like(q)
        o1 = _softmax_pv(jnp.where(first, q, zero), kt_ref, v_ref)
        o2 = _softmax_pv(jnp.where(first, zero, q), kt_ref, v_ref)
        o = o1 - lam * o2
        o = _rms_rows(o, gain_ref[...]) * (1.0 - lam_init)
        o_ref[pl.ds(off, ATT_RB), :] = o.astype(o_ref.dtype)
        return carry

    lax.fori_loop(0, ATT_TQ // ATT_RB, body, 0)


def _diff(proj, lam_params, gain, layer, B, S):
    T = proj.shape[0]
    nq = S // ATT_TQ
    q0 = 3072 // HEAD_DIM
    k0 = q0 + DIFF_HEADS
    v0 = k0 + DIFF_HEADS
    lam_init = 0.8 - 0.6 * math.exp(-0.3 * layer)
    return pl.pallas_call(
        functools.partial(_diff_kernel, lam_init=lam_init),
        grid=(B, DIFF_HEADS, nq),
        in_specs=[
            pl.BlockSpec((4, DIFF_QK_DIM), lambda b, h, i: (0, 0)),
            pl.BlockSpec((1, HEAD_DIM), lambda b, h, i: (0, 0)),
            pl.BlockSpec((ATT_TQ, HEAD_DIM), lambda b, h, i: (b * nq + i, q0 + h)),
            pl.BlockSpec((S, HEAD_DIM), lambda b, h, i: (b, k0 + h)),
            pl.BlockSpec((S, HEAD_DIM), lambda b, h, i: (b, v0 + h)),
        ],
        out_specs=pl.BlockSpec((ATT_TQ, HEAD_DIM), lambda b, h, i: (b * nq + i, h)),
        out_shape=jax.ShapeDtypeStruct((T, DIFF_HEADS * HEAD_DIM), BF16),
        scratch_shapes=[pltpu.VMEM((HEAD_DIM, S), BF16)],
        compiler_params=_params("parallel", "parallel", "arbitrary"),
        name="diff",
    )(lam_params, gain.reshape(1, HEAD_DIM), proj, proj, proj)


def _na_kernel(q_ref, k_ref, v_ref, bias_ref, o_ref, *, rows):
    blk = pl.program_id(2)
    ws = jnp.clip(blk * NA_QROWS - NA_KH // 2, 0, rows - NA_KROWS) * GRID_W
    ws = pl.multiple_of(ws, (NA_KH // 2) * GRID_W)
    k = k_ref[pl.ds(ws, NA_KROWS * GRID_W), :]
    v = v_ref[pl.ds(ws, NA_KROWS * GRID_W), :]
    s = lax.dot_general(q_ref[...], k, (((1,), (1,)), ((), ())), preferred_element_type=F32)
    s = s + bias_ref[...]
    m = jnp.max(s, axis=-1, keepdims=True)
    p = jnp.exp(s - m)
    l = jnp.sum(p, axis=-1, keepdims=True)
    o = jnp.dot(p.astype(BF16), v, preferred_element_type=F32)
    o_ref[...] = (o / l).astype(o_ref.dtype)


def _na_block_type(blk, nblk):
    return jnp.minimum(blk, 1) + blk // (nblk - 1)


def _na(proj, bias, B, S, layer):
    T = proj.shape[0]
    rows = S // GRID_W
    nblk = rows // NA_QROWS
    tq = NA_QROWS * GRID_W
    tk = NA_KROWS * GRID_W
    return pl.pallas_call(
        functools.partial(_na_kernel, rows=rows),
        grid=(B, NA_HEADS, nblk),
        in_specs=[
            pl.BlockSpec((tq, HEAD_DIM), lambda b, h, i: (b * nblk + i, h)),
            pl.BlockSpec((S, HEAD_DIM), lambda b, h, i: (b, NA_HEADS + h)),
            pl.BlockSpec((S, HEAD_DIM), lambda b, h, i: (b, 2 * NA_HEADS + h)),
            pl.BlockSpec((None, None, None, tq, tk),
                         lambda b, h, i: (layer, h, _na_block_type(i, nblk), 0, 0)),
        ],
        out_specs=pl.BlockSpec((tq, HEAD_DIM), lambda b, h, i: (b * nblk + i, h)),
        out_shape=jax.ShapeDtypeStruct((T, NA_HEADS * HEAD_DIM), BF16),
        compiler_params=_params("parallel", "parallel", "arbitrary"),
        name="na",
    )(proj, proj, proj, bias)


def _na_bias_tables(rpb, rows):
    n_dr, n_dc = 2 * NA_KH - 1, 2 * NA_KW - 1
    cq = np.arange(GRID_W)
    cs = np.clip(cq - NA_KW // 2, 0, GRID_W - NA_KW)
    ck = np.arange(GRID_W)
    valid_c = (ck[None, :] >= cs[:, None]) & (ck[None, :] < cs[:, None] + NA_KW)
    dc = ck[None, :] - cq[:, None] + (NA_KW - 1)
    onehot_c = ((dc[None] == np.arange(n_dc)[:, None, None]) & valid_c[None]).astype(np.float32)
    onehot_r = np.zeros((3, NA_QROWS, NA_KROWS, n_dr), np.float32)
    valid_r = np.zeros((3, NA_QROWS, NA_KROWS), bool)
    for t, r0 in enumerate((0, NA_QROWS, rows - NA_QROWS)):
        ws = int(np.clip(r0 - NA_KH // 2, 0, rows - NA_KROWS))
        r = r0 + np.arange(NA_QROWS)
        rs = np.clip(r - NA_KH // 2, 0, rows - NA_KH)
        kr = ws + np.arange(NA_KROWS)
        valid_r[t] = (kr[None, :] >= rs[:, None]) & (kr[None, :] < rs[:, None] + NA_KH)
        dr = kr[None, :] - r[:, None] + (NA_KH - 1)
        onehot_r[t] = (dr[..., None] == np.arange(n_dr)) & valid_r[t][..., None]
    exact = lax.Precision.HIGHEST
    colexp = jnp.einsum('lhad,dqc->lhaqc', rpb, jnp.asarray(onehot_c), precision=exact)
    full = jnp.einsum('trka,lhaqc->lhtrqkc', jnp.asarray(onehot_r), colexp, precision=exact)
    mask = (np.where(valid_r, 0.0, NEG_BIG)[:, :, None, :, None]
            + np.where(valid_c, 0.0, NEG_BIG)[None, None, :, None, :]).astype(np.float32)
    full = full + jnp.asarray(mask)[None, None]
    L, H = rpb.shape[:2]
    return full.reshape(L, H, 3, NA_QROWS * GRID_W, NA_KROWS * GRID_W)


def _outproj_kernel(h_ref, a_ref, b_ref, c_ref, w_ref, o_ref):
    na = a_ref.shape[1]
    nb = b_ref.shape[1]
    acc = jnp.dot(a_ref[...], w_ref[0:na, :].astype(BF16), preferred_element_type=F32)
    acc += jnp.dot(b_ref[...], w_ref[na:na + nb, :].astype(BF16), preferred_element_type=F32)
    acc += jnp.dot(c_ref[...], w_ref[na + nb:, :].astype(BF16), preferred_element_type=F32)
    o_ref[...] = h_ref[...] + acc


def _outproj(h, oa, ob, oc, w, layer):
    T = h.shape[0]
    return pl.pallas_call(
        _outproj_kernel,
        grid=(T // OUT_TM, D_MODEL // OUT_TN),
        in_specs=[
            pl.BlockSpec((OUT_TM, OUT_TN), lambda i, j: (i, j)),
            pl.BlockSpec((OUT_TM, oa.shape[1]), lambda i, j: (i, 0)),
            pl.BlockSpec((OUT_TM, ob.shape[1]), lambda i, j: (i, 0)),
            pl.BlockSpec((OUT_TM, oc.shape[1]), lambda i, j: (i, 0)),
            pl.BlockSpec((None, D_MODEL, OUT_TN), lambda i, j: (layer, 0, j)),
        ],
        out_specs=pl.BlockSpec((OUT_TM, OUT_TN), lambda i, j: (i, j)),
        out_shape=jax.ShapeDtypeStruct((T, D_MODEL), F32),
        compiler_params=_params("parallel", "arbitrary"),
        name="outproj",
    )(h, oa, ob, oc, w)


def _ple_kernel(h_ref, g_ref, wg_ref, p_ref, wp_ref, gf_ref, o_ref, wg_bf_ref, wp_bf_ref, *, final):
    @pl.when(pl.program_id(0) == 0)
    def _():
        wg_bf_ref[...] = wg_ref[...].astype(BF16)
        wp_bf_ref[...] = wp_ref[...].astype(BF16)

    x = h_ref[...]
    xn = _rms_rows(x, g_ref[...]).astype(BF16)
    gate = jax.nn.sigmoid(jnp.dot(xn, wg_bf_ref[...], preferred_element_type=F32))
    emb = jnp.dot(p_ref[...].astype(BF16), wp_bf_ref[...], preferred_element_type=F32)
    y = x + gate * emb
    if final:
        y = _rms_rows(y, gf_ref[...])
    o_ref[...] = y


def _ple(h, g, wg, p, wp, g_final, layer, final):
    T = h.shape[0]
    resident = pl.Buffered(1)
    return pl.pallas_call(
        functools.partial(_ple_kernel, final=final),
        grid=(T // PLE_TM,),
        in_specs=[
            pl.BlockSpec((PLE_TM, D_MODEL), lambda i: (i, 0)),
            pl.BlockSpec((1, D_MODEL), lambda i: (0, 0)),
            pl.BlockSpec((None, D_MODEL, D_MODEL), lambda i: (layer, 0, 0), pipeline_mode=resident),
            pl.BlockSpec((None, PLE_TM, PLE_DIM), lambda i: (layer, i, 0)),
            pl.BlockSpec((None, PLE_DIM, D_MODEL), lambda i: (layer, 0, 0), pipeline_mode=resident),
            pl.BlockSpec((1, D_MODEL), lambda i: (0, 0)),
        ],
        out_specs=pl.BlockSpec((PLE_TM, D_MODEL), lambda i: (i, 0)),
        out_shape=jax.ShapeDtypeStruct((T, D_MODEL), F32),
        scratch_shapes=[pltpu.VMEM((D_MODEL, D_MODEL), BF16), pltpu.VMEM((PLE_DIM, D_MODEL), BF16)],
        compiler_params=_params("arbitrary"),
        name="ple",
    )(h, g.reshape(1, D_MODEL), wg, p, wp, g_final.reshape(1, D_MODEL))


def _rope_tables(S):
    t = jnp.arange(S)
    inv_ax = jnp.power(ROPE_THETA, -jnp.arange(0, AXIAL_DIM, 2, dtype=F32) / AXIAL_DIM)
    ang_b = jnp.concatenate([(t // GRID_W).astype(F32)[:, None] * inv_ax,
                             (t % GRID_W).astype(F32)[:, None] * inv_ax], axis=-1)
    cos_b, sin_b = jnp.cos(ang_b), jnp.sin(ang_b)
    cb = jnp.concatenate([cos_b, cos_b], axis=-1)
    sb = jnp.concatenate([-sin_b, sin_b], axis=-1)
    inv_1d = jnp.power(ROPE_THETA, -jnp.arange(0, DIFF_QK_DIM, 2, dtype=F32) / DIFF_QK_DIM)
    ang_c = t.astype(F32)[:, None] * inv_1d
    cos_c, sin_c = jnp.cos(ang_c), jnp.sin(ang_c)
    zero = jnp.zeros_like(sin_c)
    cc = jnp.concatenate([cos_c] * 4, axis=-1)
    sca = jnp.concatenate([-sin_c, zero, -sin_c, zero], axis=-1)
    scb = jnp.concatenate([zero, sin_c, zero, sin_c], axis=-1)
    return cb, sb, cc, sca, scb


def kernel(x, p, g_ffn1, w1_gate, w1_up, w1_down, g_mix, w_in, na_rpb, gqa_q_gain, gqa_k_gain,
           diff_lambda, diff_subln_gain, w_out, g_ffn2, w2_gate, w2_up, w2_down, g_ple,
           w_ple_gate, w_ple_proj, g_final):
    B, S, D = x.shape
    depth = w_in.shape[0]
    T = B * S
    rows = S // GRID_W
    tables = _rope_tables(S)
    bias = _na_bias_tables(na_rpb, rows)
    p_flat = p.reshape(depth, T, PLE_DIM)
    h = x.reshape(T, D)
    for i in range(depth):
        h = _ffn(h, g_ffn1[i], w1_gate, w1_up, w1_down, i)
        proj = _proj(h, g_mix[i], w_in, gqa_q_gain[i], gqa_k_gain[i], tables, S, i)
        out_a = _na(proj, bias, B, S, i)
        out_b = _gqa(proj, B, S)
        out_c = _diff(proj, diff_lambda[i], diff_subln_gain[i], i, B, S)
        h = _outproj(h, out_a, out_b, out_c, w_out, i)
        h = _ffn(h, g_ffn2[i], w2_gate, w2_up, w2_down, i)
        h = _ple(h, g_ple[i], w_ple_gate, p_flat, w_ple_proj, g_final, i, final=(i == depth - 1))
    return h.reshape(B, S, D)
```

```python
import functools
import math

import numpy as np
import jax
import jax.numpy as jnp
from jax import lax
from jax.experimental import pallas as pl
from jax.experimental.pallas import tpu as pltpu

F32 = jnp.float32
BF16 = jnp.bfloat16

D_MODEL = 2048
EPS = 1e-6
GRID_W = 64
HEAD_DIM = 128
NA_HEADS = 4
NA_KH = 8
NA_KW = 16
GQA_HEADS = 8
GQA_KV = 2
GQA_GROUP = GQA_HEADS // GQA_KV
AXIAL_DIM = HEAD_DIM // 2
DIFF_HEADS = 4
DIFF_QK_DIM = HEAD_DIM // 2
IN_COLS = 4608
D_FF = 5632
PLE_DIM = 256
ROPE_THETA = 10000.0

NEG_BIG = -1e30
LOG2E = math.log2(math.e)
VMEM_LIMIT_BYTES = 56 * 1024 * 1024

FFN_TM = 1024
FFN_TF = 256
PROJ_TM = 1024
PROJ_TN = 512
OUT_TM = 1024
OUT_TN = 1024
PLE_TM = 512
ATT_RB = 256
ATT_KC = 256
NA_QROWS = 8
NA_KROWS = 16


def _params(*sem):
    return pltpu.CompilerParams(dimension_semantics=sem, vmem_limit_bytes=VMEM_LIMIT_BYTES)


def _rms_rows(x, g):
    return x * lax.rsqrt(jnp.mean(x * x, axis=-1, keepdims=True) + EPS) * g


def _ffn_kernel(h_ref, g_ref, wg_ref, wu_ref, wd_ref, o_ref, xn_ref):
    @pl.when(pl.program_id(1) == 0)
    def _():
        x = h_ref[...]
        xn_ref[...] = _rms_rows(x, g_ref[...]).astype(BF16)
        o_ref[...] = x

    xn = xn_ref[...]
    a = jnp.dot(xn, wg_ref[...].astype(BF16), preferred_element_type=F32)
    u = jnp.dot(xn, wu_ref[...].astype(BF16), preferred_element_type=F32)
    act = (a * jax.nn.sigmoid(a) * (0.5 * u)).astype(BF16)
    o_ref[...] += jnp.dot(act, wd_ref[...].astype(BF16), preferred_element_type=F32)


def _ffn(h, g, wg, wu, wd, layer):
    T = h.shape[0]
    return pl.pallas_call(
        _ffn_kernel,
        grid=(T // FFN_TM, D_FF // FFN_TF),
        in_specs=[
            pl.BlockSpec((FFN_TM, D_MODEL), lambda i, j: (i, 0), pipeline_mode=pl.Buffered(1)),
            pl.BlockSpec((1, D_MODEL), lambda i, j: (0, 0)),
            pl.BlockSpec((None, D_MODEL, FFN_TF), lambda i, j: (layer, 0, j)),
            pl.BlockSpec((None, D_MODEL, FFN_TF), lambda i, j: (layer, 0, j)),
            pl.BlockSpec((None, FFN_TF, D_MODEL), lambda i, j: (layer, j, 0)),
        ],
        out_specs=pl.BlockSpec((FFN_TM, D_MODEL), lambda i, j: (i, 0)),
        out_shape=jax.ShapeDtypeStruct((T, D_MODEL), F32),
        scratch_shapes=[pltpu.VMEM((FFN_TM, D_MODEL), BF16)],
        compiler_params=_params("parallel", "arbitrary"),
        name="ffn",
    )(h, g.reshape(1, D_MODEL), wg, wu, wd)


def _proj_kernel(h_ref, g_ref, w_ref, qg_ref, kg_ref, cb_ref, sb_ref, cc_ref, sca_ref, scb_ref,
                 o_ref, xn_ref):
    j = pl.program_id(1)

    @pl.when(j == 0)
    def _():
        xn_ref[...] = _rms_rows(h_ref[...], g_ref[...]).astype(BF16)

    y = jnp.dot(xn_ref[...], w_ref[...].astype(BF16), preferred_element_type=F32)
    heads = [y[:, k * HEAD_DIM:(k + 1) * HEAD_DIM] for k in range(PROJ_TN // HEAD_DIM)]

    def put(k, val):
        o_ref[:, k * HEAD_DIM:(k + 1) * HEAD_DIM] = val.astype(BF16)

    def rope_axial(x):
        return x * cb_ref[...] + pltpu.roll(x, HEAD_DIM // 2, 1) * sb_ref[...]

    def rope_pair(x):
        return (x * cc_ref[...] + pltpu.roll(x, HEAD_DIM - DIFF_QK_DIM // 2, 1) * sca_ref[...]
                + pltpu.roll(x, DIFF_QK_DIM // 2, 1) * scb_ref[...])

    @pl.when(j == 0)
    def _():
        for k in range(4):
            put(k, heads[k] * (HEAD_DIM ** -0.5 * LOG2E))

    @pl.when((j == 1) | (j == 2) | (j == 8))
    def _():
        for k in range(4):
            put(k, heads[k])

    @pl.when((j == 3) | (j == 4))
    def _():
        for k in range(4):
            put(k, rope_axial(_rms_rows(heads[k], qg_ref[...])) * (HEAD_DIM ** -0.5 * LOG2E))

    @pl.when(j == 5)
    def _():
        for k in range(2):
            put(k, rope_axial(_rms_rows(heads[k], kg_ref[...])))
        for k in range(2, 4):
            put(k, heads[k])

    @pl.when(j == 6)
    def _():
        for k in range(4):
            put(k, rope_pair(heads[k]) * (DIFF_QK_DIM ** -0.5 * LOG2E))

    @pl.when(j == 7)
    def _():
        for k in range(4):
            put(k, rope_pair(heads[k]))


def _proj(h, g, w, qg, kg, tables, S, layer):
    T = h.shape[0]
    nst = S // PROJ_TM
    tab_spec = pl.BlockSpec((PROJ_TM, HEAD_DIM), lambda i, j: (i % nst, 0))
    vec_spec = pl.BlockSpec((1, HEAD_DIM), lambda i, j: (0, 0))
    return pl.pallas_call(
        _proj_kernel,
        grid=(T // PROJ_TM, IN_COLS // PROJ_TN),
        in_specs=[
            pl.BlockSpec((PROJ_TM, D_MODEL), lambda i, j: (i, 0)),
            pl.BlockSpec((1, D_MODEL), lambda i, j: (0, 0)),
            pl.BlockSpec((None, D_MODEL, PROJ_TN), lambda i, j: (layer, 0, j)),
            vec_spec, vec_spec, tab_spec, tab_spec, tab_spec, tab_spec, tab_spec,
        ],
        out_specs=pl.BlockSpec((PROJ_TM, PROJ_TN), lambda i, j: (i, j)),
        out_shape=jax.ShapeDtypeStruct((T, IN_COLS), BF16),
        scratch_shapes=[pltpu.VMEM((PROJ_TM, D_MODEL), BF16)],
        compiler_params=_params("parallel", "arbitrary"),
        name="proj",
    )(h, g.reshape(1, D_MODEL), w, qg.reshape(1, HEAD_DIM), kg.reshape(1, HEAD_DIM), *tables)


def _transpose_keys(k_ref, kt_ref):
    S = k_ref.shape[0]
    step = 512
    for c in range(S // step):
        blk = k_ref[c * step:(c + 1) * step, :].astype(F32)
        kt_ref[:, c * step:(c + 1) * step] = blk.T.astype(BF16)


def _attention_rows(q_ref, kt_ref, v_ref, s_refs, emit):
    npair = q_ref.shape[0] // (2 * ATT_RB)
    S = kt_ref.shape[1]

    def scores(r, slot):
        off = pl.multiple_of(r * ATT_RB, ATT_RB)
        s = jnp.dot(q_ref[pl.ds(off, ATT_RB), :], kt_ref[...], preferred_element_type=F32)
        s_refs[slot][...] = s
        return jnp.max(s, axis=-1, keepdims=True)

    def consume(slot, m):
        acc = jnp.zeros((ATT_RB, HEAD_DIM), F32)
        l = jnp.zeros((ATT_RB, 1), F32)
        for c in range(S // ATT_KC):
            p = jnp.exp2(s_refs[slot][:, c * ATT_KC:(c + 1) * ATT_KC] - m)
            l = l + jnp.sum(p, axis=-1, keepdims=True)
            acc = acc + jnp.dot(p.astype(BF16), v_ref[c * ATT_KC:(c + 1) * ATT_KC, :],
                                preferred_element_type=F32)
        return acc / l

    def body(i, m_even):
        m_odd = scores(2 * i + 1, 1)
        o_even = consume(0, m_even)
        m_next = scores(2 * i + 2, 0)
        o_odd = consume(1, m_odd)
        emit(i, o_even, o_odd)
        return m_next

    m_even = lax.fori_loop(0, npair - 1, body, scores(0, 0))
    m_odd = scores(2 * npair - 1, 1)
    o_even = consume(0, m_even)
    o_odd = consume(1, m_odd)
    emit(npair - 1, o_even, o_odd)


def _gqa_kernel(q_ref, k_ref, v_ref, o_ref, kt_ref, s0_ref, s1_ref, qs_ref, os_ref):
    S = k_ref.shape[0]
    _transpose_keys(k_ref, kt_ref)
    for g in range(GQA_GROUP):
        qs_ref[g * S:(g + 1) * S, :] = q_ref[:, g * HEAD_DIM:(g + 1) * HEAD_DIM]

    def emit(i, o_even, o_odd):
        off = pl.multiple_of(i * 2 * ATT_RB, 2 * ATT_RB)
        os_ref[pl.ds(off, ATT_RB), :] = o_even.astype(os_ref.dtype)
        os_ref[pl.ds(off + ATT_RB, ATT_RB), :] = o_odd.astype(os_ref.dtype)

    _attention_rows(qs_ref, kt_ref, v_ref, (s0_ref, s1_ref), emit)
    for g in range(GQA_GROUP):
        o_ref[:, g * HEAD_DIM:(g + 1) * HEAD_DIM] = os_ref[g * S:(g + 1) * S, :]


def _gqa(proj, B, S):
    T = proj.shape[0]
    gw = GQA_GROUP * HEAD_DIM
    q0 = 1536 // gw
    k0 = (1536 + GQA_HEADS * HEAD_DIM) // HEAD_DIM
    v0 = k0 + GQA_KV
    return pl.pallas_call(
        _gqa_kernel,
        grid=(B, GQA_KV),
        in_specs=[
            pl.BlockSpec((S, gw), lambda b, kh: (b, q0 + kh)),
            pl.BlockSpec((S, HEAD_DIM), lambda b, kh: (b, k0 + kh)),
            pl.BlockSpec((S, HEAD_DIM), lambda b, kh: (b, v0 + kh)),
        ],
        out_specs=pl.BlockSpec((S, gw), lambda b, kh: (b, kh)),
        out_shape=jax.ShapeDtypeStruct((T, GQA_HEADS * HEAD_DIM), BF16),
        scratch_shapes=[pltpu.VMEM((HEAD_DIM, S), BF16),
                        pltpu.VMEM((ATT_RB, S), F32), pltpu.VMEM((ATT_RB, S), F32),
                        pltpu.VMEM((GQA_GROUP * S, HEAD_DIM), BF16),
                        pltpu.VMEM((GQA_GROUP * S, HEAD_DIM), BF16)],
        compiler_params=_params("parallel", "parallel"),
        name="gqa",
    )(proj, proj, proj)


def _diff_kernel(lam_ref, gain_ref, q_ref, k_ref, v_ref, o_ref, kt_ref, s0_ref, s1_ref, qc_ref, *, lam_init):
    _transpose_keys(k_ref, kt_ref)

    S = q_ref.shape[0]
    lane = lax.broadcasted_iota(jnp.int32, (ATT_RB, HEAD_DIM), 1)
    first = lane < DIFF_QK_DIM
    for r in range(S // ATT_RB):
        q = q_ref[r * ATT_RB:(r + 1) * ATT_RB, :]
        zero = jnp.zeros_like(q)
        qc_ref[2 * r * ATT_RB:(2 * r + 1) * ATT_RB, :] = jnp.where(first, q, zero)
        qc_ref[(2 * r + 1) * ATT_RB:(2 * r + 2) * ATT_RB, :] = jnp.where(first, zero, q)

    lp = lam_ref[...]
    lam = (jnp.exp(jnp.sum(lp[0:1] * lp[1:2], axis=-1, keepdims=True))
           - jnp.exp(jnp.sum(lp[2:3] * lp[3:4], axis=-1, keepdims=True)) + lam_init)

    def emit(i, o1, o2):
        off = pl.multiple_of(i * ATT_RB, ATT_RB)
        o = o1 - lam * o2
        o = _rms_rows(o, gain_ref[...]) * (1.0 - lam_init)
        o_ref[pl.ds(off, ATT_RB), :] = o.astype(o_ref.dtype)

    _attention_rows(qc_ref, kt_ref, v_ref, (s0_ref, s1_ref), emit)


def _diff(proj, lam_params, gain, layer, B, S):
    T = proj.shape[0]
    q0 = 3072 // HEAD_DIM
    k0 = q0 + DIFF_HEADS
    v0 = k0 + DIFF_HEADS
    lam_init = 0.8 - 0.6 * math.exp(-0.3 * layer)
    return pl.pallas_call(
        functools.partial(_diff_kernel, lam_init=lam_init),
        grid=(B, DIFF_HEADS),
        in_specs=[
            pl.BlockSpec((4, DIFF_QK_DIM), lambda b, h: (0, 0)),
            pl.BlockSpec((1, HEAD_DIM), lambda b, h: (0, 0)),
            pl.BlockSpec((S, HEAD_DIM), lambda b, h: (b, q0 + h)),
            pl.BlockSpec((S, HEAD_DIM), lambda b, h: (b, k0 + h)),
            pl.BlockSpec((S, HEAD_DIM), lambda b, h: (b, v0 + h)),
        ],
        out_specs=pl.BlockSpec((S, HEAD_DIM), lambda b, h: (b, h)),
        out_shape=jax.ShapeDtypeStruct((T, DIFF_HEADS * HEAD_DIM), BF16),
        scratch_shapes=[pltpu.VMEM((HEAD_DIM, S), BF16),
                        pltpu.VMEM((ATT_RB, S), F32), pltpu.VMEM((ATT_RB, S), F32),
                        pltpu.VMEM((2 * S, HEAD_DIM), BF16)],
        compiler_params=_params("parallel", "parallel"),
        name="diff",
    )(lam_params, gain.reshape(1, HEAD_DIM), proj, proj, proj)


def _na_kernel(q_ref, k_ref, v_ref, bias_ref, o_ref, *, rows):
    blk = pl.program_id(2)
    ws = jnp.clip(blk * NA_QROWS - NA_KH // 2, 0, rows - NA_KROWS) * GRID_W
    ws = pl.multiple_of(ws, (NA_KH // 2) * GRID_W)
    k = k_ref[pl.ds(ws, NA_KROWS * GRID_W), :]
    v = v_ref[pl.ds(ws, NA_KROWS * GRID_W), :]
    s = lax.dot_general(q_ref[...], k, (((1,), (1,)), ((), ())), preferred_element_type=F32)
    s = s + bias_ref[...]
    m = jnp.max(s, axis=-1, keepdims=True)
    p = jnp.exp2(s - m)
    l = jnp.sum(p, axis=-1, keepdims=True)
    o = jnp.dot(p.astype(BF16), v, preferred_element_type=F32)
    o_ref[...] = (o / l).astype(o_ref.dtype)


def _na_block_type(blk, nblk):
    return jnp.minimum(blk, 1) + blk // (nblk - 1)


def _na(proj, bias, B, S, layer):
    T = proj.shape[0]
    rows = S // GRID_W
    nblk = rows // NA_QROWS
    tq = NA_QROWS * GRID_W
    tk = NA_KROWS * GRID_W
    return pl.pallas_call(
        functools.partial(_na_kernel, rows=rows),
        grid=(B, NA_HEADS, nblk),
        in_specs=[
            pl.BlockSpec((tq, HEAD_DIM), lambda b, h, i: (b * nblk + i, h)),
            pl.BlockSpec((S, HEAD_DIM), lambda b, h, i: (b, NA_HEADS + h)),
            pl.BlockSpec((S, HEAD_DIM), lambda b, h, i: (b, 2 * NA_HEADS + h)),
            pl.BlockSpec((None, None, None, tq, tk),
                         lambda b, h, i: (layer, h, _na_block_type(i, nblk), 0, 0)),
        ],
        out_specs=pl.BlockSpec((tq, HEAD_DIM), lambda b, h, i: (b * nblk + i, h)),
        out_shape=jax.ShapeDtypeStruct((T, NA_HEADS * HEAD_DIM), BF16),
        compiler_params=_params("parallel", "parallel", "arbitrary"),
        name="na",
    )(proj, proj, proj, bias)


def _na_bias_tables(rpb, rows):
    n_dr, n_dc = 2 * NA_KH - 1, 2 * NA_KW - 1
    cq = np.arange(GRID_W)
    cs = np.clip(cq - NA_KW // 2, 0, GRID_W - NA_KW)
    ck = np.arange(GRID_W)
    valid_c = (ck[None, :] >= cs[:, None]) & (ck[None, :] < cs[:, None] + NA_KW)
    dc = ck[None, :] - cq[:, None] + (NA_KW - 1)
    onehot_c = ((dc[None] == np.arange(n_dc)[:, None, None]) & valid_c[None]).astype(np.float32)
    onehot_r = np.zeros((3, NA_QROWS, NA_KROWS, n_dr), np.float32)
    valid_r = np.zeros((3, NA_QROWS, NA_KROWS), bool)
    for t, r0 in enumerate((0, NA_QROWS, rows - NA_QROWS)):
        ws = int(np.clip(r0 - NA_KH // 2, 0, rows - NA_KROWS))
        r = r0 + np.arange(NA_QROWS)
        rs = np.clip(r - NA_KH // 2, 0, rows - NA_KH)
        kr = ws + np.arange(NA_KROWS)
        valid_r[t] = (kr[None, :] >= rs[:, None]) & (kr[None, :] < rs[:, None] + NA_KH)
        dr = kr[None, :] - r[:, None] + (NA_KH - 1)
        onehot_r[t] = (dr[..., None] == np.arange(n_dr)) & valid_r[t][..., None]
    exact = lax.Precision.HIGHEST
    colexp = jnp.einsum('lhad,dqc->lhaqc', rpb, jnp.asarray(onehot_c), precision=exact)
    full = jnp.einsum('trka,lhaqc->lhtrqkc', jnp.asarray(onehot_r), colexp, precision=exact)
    mask = (np.where(valid_r, 0.0, NEG_BIG)[:, :, None, :, None]
            + np.where(valid_c, 0.0, NEG_BIG)[None, None, :, None, :]).astype(np.float32)
    full = (full + jnp.asarray(mask)[None, None]) * LOG2E
    L, H = rpb.shape[:2]
    return full.reshape(L, H, 3, NA_QROWS * GRID_W, NA_KROWS * GRID_W)


def _outproj_kernel(h_ref, a_ref, b_ref, c_ref, w_ref, o_ref):
    na = a_ref.shape[1]
    nb = b_ref.shape[1]
    acc = jnp.dot(a_ref[...], w_ref[0:na, :].astype(BF16), preferred_element_type=F32)
    acc += jnp.dot(b_ref[...], w_ref[na:na + nb, :].astype(BF16), preferred_element_type=F32)
    acc += jnp.dot(c_ref[...], w_ref[na + nb:, :].astype(BF16), preferred_element_type=F32)
    o_ref[...] = h_ref[...] + acc


def _outproj(h, oa, ob, oc, w, layer):
    T = h.shape[0]
    return pl.pallas_call(
        _outproj_kernel,
        grid=(T // OUT_TM, D_MODEL // OUT_TN),
        in_specs=[
            pl.BlockSpec((OUT_TM, OUT_TN), lambda i, j: (i, j)),
            pl.BlockSpec((OUT_TM, oa.shape[1]), lambda i, j: (i, 0)),
            pl.BlockSpec((OUT_TM, ob.shape[1]), lambda i, j: (i, 0)),
            pl.BlockSpec((OUT_TM, oc.shape[1]), lambda i, j: (i, 0)),
            pl.BlockSpec((None, D_MODEL, OUT_TN), lambda i, j: (layer, 0, j)),
        ],
        out_specs=pl.BlockSpec((OUT_TM, OUT_TN), lambda i, j: (i, j)),
        out_shape=jax.ShapeDtypeStruct((T, D_MODEL), F32),
        compiler_params=_params("parallel", "arbitrary"),
        name="outproj",
    )(h, oa, ob, oc, w)


def _ple_kernel(h_ref, g_ref, wg_ref, p_ref, wp_ref, gf_ref, o_ref, wg_bf_ref, wp_bf_ref, *, final):
    @pl.when(pl.program_id(0) == 0)
    def _():
        wg_bf_ref[...] = wg_ref[...].astype(BF16)
        wp_bf_ref[...] = wp_ref[...].astype(BF16)

    x = h_ref[...]
    xn = _rms_rows(x, g_ref[...]).astype(BF16)
    gate = jax.nn.sigmoid(jnp.dot(xn, wg_bf_ref[...], preferred_element_type=F32))
    emb = jnp.dot(p_ref[...].astype(BF16), wp_bf_ref[...], preferred_element_type=F32)
    y = x + gate * emb
    if final:
        y = _rms_rows(y, gf_ref[...])
    o_ref[...] = y


def _ple(h, g, wg, p, wp, g_final, layer, final):
    T = h.shape[0]
    resident = pl.Buffered(1)
    return pl.pallas_call(
        functools.partial(_ple_kernel, final=final),
        grid=(T // PLE_TM,),
        in_specs=[
            pl.BlockSpec((PLE_TM, D_MODEL), lambda i: (i, 0)),
            pl.BlockSpec((1, D_MODEL), lambda i: (0, 0)),
            pl.BlockSpec((None, D_MODEL, D_MODEL), lambda i: (layer, 0, 0), pipeline_mode=resident),
            pl.BlockSpec((None, PLE_TM, PLE_DIM), lambda i: (layer, i, 0)),
            pl.BlockSpec((None, PLE_DIM, D_MODEL), lambda i: (layer, 0, 0), pipeline_mode=resident),
            pl.BlockSpec((1, D_MODEL), lambda i: (0, 0)),
        ],
        out_specs=pl.BlockSpec((PLE_TM, D_MODEL), lambda i: (i, 0)),
        out_shape=jax.ShapeDtypeStruct((T, D_MODEL), F32),
        scratch_shapes=[pltpu.VMEM((D_MODEL, D_MODEL), BF16), pltpu.VMEM((PLE_DIM, D_MODEL), BF16)],
        compiler_params=_params("arbitrary"),
        name="ple",
    )(h, g.reshape(1, D_MODEL), wg, p, wp, g_final.reshape(1, D_MODEL))


def _rope_tables(S):
    t = jnp.arange(S)
    inv_ax = jnp.power(ROPE_THETA, -jnp.arange(0, AXIAL_DIM, 2, dtype=F32) / AXIAL_DIM)
    ang_b = jnp.concatenate([(t // GRID_W).astype(F32)[:, None] * inv_ax,
                             (t % GRID_W).astype(F32)[:, None] * inv_ax], axis=-1)
    cos_b, sin_b = jnp.cos(ang_b), jnp.sin(ang_b)
    cb = jnp.concatenate([cos_b, cos_b], axis=-1)
    sb = jnp.concatenate([-sin_b, sin_b], axis=-1)
    inv_1d = jnp.power(ROPE_THETA, -jnp.arange(0, DIFF_QK_DIM, 2, dtype=F32) / DIFF_QK_DIM)
    ang_c = t.astype(F32)[:, None] * inv_1d
    cos_c, sin_c = jnp.cos(ang_c), jnp.sin(ang_c)
    zero = jnp.zeros_like(sin_c)
    cc = jnp.concatenate([cos_c] * 4, axis=-1)
    sca = jnp.concatenate([-sin_c, zero, -sin_c, zero], axis=-1)
    scb = jnp.concatenate([zero, sin_c, zero, sin_c], axis=-1)
    return cb, sb, cc, sca, scb


def kernel(x, p, g_ffn1, w1_gate, w1_up, w1_down, g_mix, w_in, na_rpb, gqa_q_gain, gqa_k_gain,
           diff_lambda, diff_subln_gain, w_out, g_ffn2, w2_gate, w2_up, w2_down, g_ple,
           w_ple_gate, w_ple_proj, g_final):
    B, S, D = x.shape
    depth = w_in.shape[0]
    T = B * S
    rows = S // GRID_W
    tables = _rope_tables(S)
    bias = _na_bias_tables(na_rpb, rows)
    p_flat = p.reshape(depth, T, PLE_DIM)
    h = x.reshape(T, D)
    for i in range(depth):
        h = _ffn(h, g_ffn1[i], w1_gate, w1_up, w1_down, i)
        proj = _proj(h, g_mix[i], w_in, gqa_q_gain[i], gqa_k_gain[i], tables, S, i)
        out_a = _na(proj, bias, B, S, i)
        out_b = _gqa(proj, B, S)
        out_c = _diff(proj, diff_lambda[i], diff_subln_gain[i], i, B, S)
        h = _outproj(h, out_a, out_b, out_c, w_out, i)
        h = _ffn(h, g_ffn2[i], w2_gate, w2_up, w2_down, i)
        h = _ple(h, g_ple[i], w_ple_gate, p_flat, w_ple_proj, g_final, i, final=(i == depth - 1))
    return h.reshape(B, S, D)
```

```python
import functools
import math

import numpy as np
import jax
import jax.numpy as jnp
from jax import lax
from jax.experimental import pallas as pl
from jax.experimental.pallas import tpu as pltpu

F32 = jnp.float32
BF16 = jnp.bfloat16

D_MODEL = 2048
EPS = 1e-6
GRID_W = 64
HEAD_DIM = 128
NA_HEADS = 4
NA_KH = 8
NA_KW = 16
GQA_HEADS = 8
GQA_KV = 2
GQA_GROUP = GQA_HEADS // GQA_KV
AXIAL_DIM = HEAD_DIM // 2
DIFF_HEADS = 4
DIFF_QK_DIM = HEAD_DIM // 2
IN_COLS = 4608
D_FF = 5632
PLE_DIM = 256
ROPE_THETA = 10000.0

NEG_BIG = -1e30
LOG2E = math.log2(math.e)
VMEM_LIMIT_BYTES = 56 * 1024 * 1024

FFN_TM = 1024
FFN_TF = 256
PROJ_TM = 1024
PROJ_TN = 512
OUT_TM = 1024
OUT_TN = 1024
PLE_TM = 512
ATT_RB = 512
ATT_KC = 512
NA_QROWS = 8
NA_KROWS = 16


def _params(*sem):
    return pltpu.CompilerParams(dimension_semantics=sem, vmem_limit_bytes=VMEM_LIMIT_BYTES)


def _rms_rows(x, g):
    return x * lax.rsqrt(jnp.mean(x * x, axis=-1, keepdims=True) + EPS) * g


def _ffn_kernel(h_ref, g_ref, wg_ref, wu_ref, wd_ref, o_ref, xn_ref):
    @pl.when(pl.program_id(1) == 0)
    def _():
        x = h_ref[...]
        xn_ref[...] = _rms_rows(x, g_ref[...]).astype(BF16)
        o_ref[...] = x

    xn = xn_ref[...]
    a = jnp.dot(xn, wg_ref[...].astype(BF16), preferred_element_type=F32)
    u = jnp.dot(xn, wu_ref[...].astype(BF16), preferred_element_type=F32)
    act = (a * jax.nn.sigmoid(a) * (0.5 * u)).astype(BF16)
    o_ref[...] += jnp.dot(act, wd_ref[...].astype(BF16), preferred_element_type=F32)


def _ffn(h, g, wg, wu, wd, layer):
    T = h.shape[0]
    return pl.pallas_call(
        _ffn_kernel,
        grid=(T // FFN_TM, D_FF // FFN_TF),
        in_specs=[
            pl.BlockSpec((FFN_TM, D_MODEL), lambda i, j: (i, 0), pipeline_mode=pl.Buffered(1)),
            pl.BlockSpec((1, D_MODEL), lambda i, j: (0, 0)),
            pl.BlockSpec((None, D_MODEL, FFN_TF), lambda i, j: (layer, 0, j)),
            pl.BlockSpec((None, D_MODEL, FFN_TF), lambda i, j: (layer, 0, j)),
            pl.BlockSpec((None, FFN_TF, D_MODEL), lambda i, j: (layer, j, 0)),
        ],
        out_specs=pl.BlockSpec((FFN_TM, D_MODEL), lambda i, j: (i, 0)),
        out_shape=jax.ShapeDtypeStruct((T, D_MODEL), F32),
        scratch_shapes=[pltpu.VMEM((FFN_TM, D_MODEL), BF16)],
        compiler_params=_params("parallel", "arbitrary"),
        name="ffn",
    )(h, g.reshape(1, D_MODEL), wg, wu, wd)


def _proj_kernel(h_ref, g_ref, w_ref, qg_ref, kg_ref, cb_ref, sb_ref, cc_ref, sca_ref, scb_ref,
                 o_ref, xn_ref):
    j = pl.program_id(1)

    @pl.when(j == 0)
    def _():
        xn_ref[...] = _rms_rows(h_ref[...], g_ref[...]).astype(BF16)

    y = jnp.dot(xn_ref[...], w_ref[...].astype(BF16), preferred_element_type=F32)
    heads = [y[:, k * HEAD_DIM:(k + 1) * HEAD_DIM] for k in range(PROJ_TN // HEAD_DIM)]

    def put(k, val):
        o_ref[:, k * HEAD_DIM:(k + 1) * HEAD_DIM] = val.astype(BF16)

    def rope_axial(x):
        return x * cb_ref[...] + pltpu.roll(x, HEAD_DIM // 2, 1) * sb_ref[...]

    def rope_pair(x):
        return (x * cc_ref[...] + pltpu.roll(x, HEAD_DIM - DIFF_QK_DIM // 2, 1) * sca_ref[...]
                + pltpu.roll(x, DIFF_QK_DIM // 2, 1) * scb_ref[...])

    @pl.when(j == 0)
    def _():
        for k in range(4):
            put(k, heads[k] * (HEAD_DIM ** -0.5 * LOG2E))

    @pl.when((j == 1) | (j == 2) | (j == 8))
    def _():
        for k in range(4):
            put(k, heads[k])

    @pl.when((j == 3) | (j == 4))
    def _():
        for k in range(4):
            put(k, rope_axial(_rms_rows(heads[k], qg_ref[...])) * (HEAD_DIM ** -0.5 * LOG2E))

    @pl.when(j == 5)
    def _():
        for k in range(2):
            put(k, rope_axial(_rms_rows(heads[k], kg_ref[...])))
        for k in range(2, 4):
            put(k, heads[k])

    @pl.when(j == 6)
    def _():
        for k in range(4):
            put(k, rope_pair(heads[k]) * (DIFF_QK_DIM ** -0.5 * LOG2E))

    @pl.when(j == 7)
    def _():
        for k in range(4):
            put(k, rope_pair(heads[k]))


def _proj(h, g, w, qg, kg, tables, S, layer):
    T = h.shape[0]
    nst = S // PROJ_TM
    tab_spec = pl.BlockSpec((PROJ_TM, HEAD_DIM), lambda i, j: (i % nst, 0))
    vec_spec = pl.BlockSpec((1, HEAD_DIM), lambda i, j: (0, 0))
    return pl.pallas_call(
        _proj_kernel,
        grid=(T // PROJ_TM, IN_COLS // PROJ_TN),
        in_specs=[
            pl.BlockSpec((PROJ_TM, D_MODEL), lambda i, j: (i, 0)),
            pl.BlockSpec((1, D_MODEL), lambda i, j: (0, 0)),
            pl.BlockSpec((None, D_MODEL, PROJ_TN), lambda i, j: (layer, 0, j)),
            vec_spec, vec_spec, tab_spec, tab_spec, tab_spec, tab_spec, tab_spec,
        ],
        out_specs=pl.BlockSpec((PROJ_TM, PROJ_TN), lambda i, j: (i, j)),
        out_shape=jax.ShapeDtypeStruct((T, IN_COLS), BF16),
        scratch_shapes=[pltpu.VMEM((PROJ_TM, D_MODEL), BF16)],
        compiler_params=_params("parallel", "arbitrary"),
        name="proj",
    )(h, g.reshape(1, D_MODEL), w, qg.reshape(1, HEAD_DIM), kg.reshape(1, HEAD_DIM), *tables)


def _transpose_values(v_ref, vt_ref):
    S = v_ref.shape[0]
    step = 512
    for c in range(S // step):
        blk = v_ref[c * step:(c + 1) * step, :].astype(F32)
        vt_ref[:, c * step:(c + 1) * step] = blk.T.astype(BF16)


def _attention_rows(q_ref, k_ref, vt_ref, s_refs, emit):
    npair = q_ref.shape[0] // (2 * ATT_RB)
    S = k_ref.shape[0]

    def scores(r, slot):
        off = pl.multiple_of(r * ATT_RB, ATT_RB)
        st = lax.dot_general(k_ref[...], q_ref[pl.ds(off, ATT_RB), :], (((1,), (1,)), ((), ())),
                             preferred_element_type=F32)
        s_refs[slot][...] = st
        return jnp.max(st, axis=0, keepdims=True)

    def consume(slot, m):
        acc = jnp.zeros((HEAD_DIM, ATT_RB), F32)
        l = jnp.zeros((1, ATT_RB), F32)
        for c in range(S // ATT_KC):
            p = jnp.exp2(s_refs[slot][c * ATT_KC:(c + 1) * ATT_KC, :] - m)
            l = l + jnp.sum(p, axis=0, keepdims=True)
            acc = acc + jnp.dot(vt_ref[:, c * ATT_KC:(c + 1) * ATT_KC], p.astype(BF16),
                                preferred_element_type=F32)
        return (acc / l).T

    def body(i, m_even):
        m_odd = scores(2 * i + 1, 1)
        o_even = consume(0, m_even)
        m_next = scores(2 * i + 2, 0)
        o_odd = consume(1, m_odd)
        emit(i, o_even, o_odd)
        return m_next

    m_even = lax.fori_loop(0, npair - 1, body, scores(0, 0))
    m_odd = scores(2 * npair - 1, 1)
    o_even = consume(0, m_even)
    o_odd = consume(1, m_odd)
    emit(npair - 1, o_even, o_odd)


def _gqa_kernel(q_ref, k_ref, v_ref, o_ref, vt_ref, s0_ref, s1_ref, qs_ref, os_ref):
    S = k_ref.shape[0]
    _transpose_values(v_ref, vt_ref)
    for g in range(GQA_GROUP):
        qs_ref[g * S:(g + 1) * S, :] = q_ref[:, g * HEAD_DIM:(g + 1) * HEAD_DIM]

    def emit(i, o_even, o_odd):
        off = pl.multiple_of(i * 2 * ATT_RB, 2 * ATT_RB)
        os_ref[pl.ds(off, ATT_RB), :] = o_even.astype(os_ref.dtype)
        os_ref[pl.ds(off + ATT_RB, ATT_RB), :] = o_odd.astype(os_ref.dtype)

    _attention_rows(qs_ref, k_ref, vt_ref, (s0_ref, s1_ref), emit)
    for g in range(GQA_GROUP):
        o_ref[:, g * HEAD_DIM:(g + 1) * HEAD_DIM] = os_ref[g * S:(g + 1) * S, :]


def _gqa(proj, B, S):
    T = proj.shape[0]
    gw = GQA_GROUP * HEAD_DIM
    q0 = 1536 // gw
    k0 = (1536 + GQA_HEADS * HEAD_DIM) // HEAD_DIM
    v0 = k0 + GQA_KV
    return pl.pallas_call(
        _gqa_kernel,
        grid=(B, GQA_KV),
        in_specs=[
            pl.BlockSpec((S, gw), lambda b, kh: (b, q0 + kh)),
            pl.BlockSpec((S, HEAD_DIM), lambda b, kh: (b, k0 + kh)),
            pl.BlockSpec((S, HEAD_DIM), lambda b, kh: (b, v0 + kh)),
        ],
        out_specs=pl.BlockSpec((S, gw), lambda b, kh: (b, kh)),
        out_shape=jax.ShapeDtypeStruct((T, GQA_HEADS * HEAD_DIM), BF16),
        scratch_shapes=[pltpu.VMEM((HEAD_DIM, S), BF16),
                        pltpu.VMEM((S, ATT_RB), F32), pltpu.VMEM((S, ATT_RB), F32),
                        pltpu.VMEM((GQA_GROUP * S, HEAD_DIM), BF16),
                        pltpu.VMEM((GQA_GROUP * S, HEAD_DIM), BF16)],
        compiler_params=_params("parallel", "parallel"),
        name="gqa",
    )(proj, proj, proj)


def _diff_kernel(lam_ref, gain_ref, q_ref, k_ref, v_ref, o_ref, vt_ref, s0_ref, s1_ref, qc_ref, *, lam_init):
    _transpose_values(v_ref, vt_ref)

    S = q_ref.shape[0]
    lane = lax.broadcasted_iota(jnp.int32, (ATT_RB, HEAD_DIM), 1)
    first = lane < DIFF_QK_DIM
    for r in range(S // ATT_RB):
        q = q_ref[r * ATT_RB:(r + 1) * ATT_RB, :]
        zero = jnp.zeros_like(q)
        qc_ref[2 * r * ATT_RB:(2 * r + 1) * ATT_RB, :] = jnp.where(first, q, zero)
        qc_ref[(2 * r + 1) * ATT_RB:(2 * r + 2) * ATT_RB, :] = jnp.where(first, zero, q)

    lp = lam_ref[...]
    lam = (jnp.exp(jnp.sum(lp[0:1] * lp[1:2], axis=-1, keepdims=True))
           - jnp.exp(jnp.sum(lp[2:3] * lp[3:4], axis=-1, keepdims=True)) + lam_init)

    def emit(i, o1, o2):
        off = pl.multiple_of(i * ATT_RB, ATT_RB)
        o = o1 - lam * o2
        o = _rms_rows(o, gain_ref[...]) * (1.0 - lam_init)
        o_ref[pl.ds(off, ATT_RB), :] = o.astype(o_ref.dtype)

    _attention_rows(qc_ref, k_ref, vt_ref, (s0_ref, s1_ref), emit)


def _diff(proj, lam_params, gain, layer, B, S):
    T = proj.shape[0]
    q0 = 3072 // HEAD_DIM
    k0 = q0 + DIFF_HEADS
    v0 = k0 + DIFF_HEADS
    lam_init = 0.8 - 0.6 * math.exp(-0.3 * layer)
    return pl.pallas_call(
        functools.partial(_diff_kernel, lam_init=lam_init),
        grid=(B, DIFF_HEADS),
        in_specs=[
            pl.BlockSpec((4, DIFF_QK_DIM), lambda b, h: (0, 0)),
            pl.BlockSpec((1, HEAD_DIM), lambda b, h: (0, 0)),
            pl.BlockSpec((S, HEAD_DIM), lambda b, h: (b, q0 + h)),
            pl.BlockSpec((S, HEAD_DIM), lambda b, h: (b, k0 + h)),
            pl.BlockSpec((S, HEAD_DIM), lambda b, h: (b, v0 + h)),
        ],
        out_specs=pl.BlockSpec((S, HEAD_DIM), lambda b, h: (b, h)),
        out_shape=jax.ShapeDtypeStruct((T, DIFF_HEADS * HEAD_DIM), BF16),
        scratch_shapes=[pltpu.VMEM((HEAD_DIM, S), BF16),
                        pltpu.VMEM((S, ATT_RB), F32), pltpu.VMEM((S, ATT_RB), F32),
                        pltpu.VMEM((2 * S, HEAD_DIM), BF16)],
        compiler_params=_params("parallel", "parallel"),
        name="diff",
    )(lam_params, gain.reshape(1, HEAD_DIM), proj, proj, proj)


def _na_kernel(q_ref, k_ref, v_ref, bias_ref, o_ref, *, rows):
    blk = pl.program_id(2)
    ws = jnp.clip(blk * NA_QROWS - NA_KH // 2, 0, rows - NA_KROWS) * GRID_W
    ws = pl.multiple_of(ws, (NA_KH // 2) * GRID_W)
    k = k_ref[pl.ds(ws, NA_KROWS * GRID_W), :]
    v = v_ref[pl.ds(ws, NA_KROWS * GRID_W), :]
    s = lax.dot_general(q_ref[...], k, (((1,), (1,)), ((), ())), preferred_element_type=F32)
    s = s + bias_ref[...]
    m = jnp.max(s, axis=-1, keepdims=True)
    p = jnp.exp2(s - m)
    l = jnp.sum(p, axis=-1, keepdims=True)
    o = jnp.dot(p.astype(BF16), v, preferred_element_type=F32)
    o_ref[...] = (o / l).astype(o_ref.dtype)


def _na_block_type(blk, nblk):
    return jnp.minimum(blk, 1) + blk // (nblk - 1)


def _na(proj, bias, B, S, layer):
    T = proj.shape[0]
    rows = S // GRID_W
    nblk = rows // NA_QROWS
    tq = NA_QROWS * GRID_W
    tk = NA_KROWS * GRID_W
    return pl.pallas_call(
        functools.partial(_na_kernel, rows=rows),
        grid=(B, NA_HEADS, nblk),
        in_specs=[
            pl.BlockSpec((tq, HEAD_DIM), lambda b, h, i: (b * nblk + i, h)),
            pl.BlockSpec((S, HEAD_DIM), lambda b, h, i: (b, NA_HEADS + h)),
            pl.BlockSpec((S, HEAD_DIM), lambda b, h, i: (b, 2 * NA_HEADS + h)),
            pl.BlockSpec((None, None, None, tq, tk),
                         lambda b, h, i: (layer, h, _na_block_type(i, nblk), 0, 0)),
        ],
        out_specs=pl.BlockSpec((tq, HEAD_DIM), lambda b, h, i: (b * nblk + i, h)),
        out_shape=jax.ShapeDtypeStruct((T, NA_HEADS * HEAD_DIM), BF16),
        compiler_params=_params("parallel", "parallel", "arbitrary"),
        name="na",
    )(proj, proj, proj, bias)


def _na_bias_tables(rpb, rows):
    n_dr, n_dc = 2 * NA_KH - 1, 2 * NA_KW - 1
    cq = np.arange(GRID_W)
    cs = np.clip(cq - NA_KW // 2, 0, GRID_W - NA_KW)
    ck = np.arange(GRID_W)
    valid_c = (ck[None, :] >= cs[:, None]) & (ck[None, :] < cs[:, None] + NA_KW)
    dc = ck[None, :] - cq[:, None] + (NA_KW - 1)
    onehot_c = ((dc[None] == np.arange(n_dc)[:, None, None]) & valid_c[None]).astype(np.float32)
    onehot_r = np.zeros((3, NA_QROWS, NA_KROWS, n_dr), np.float32)
    valid_r = np.zeros((3, NA_QROWS, NA_KROWS), bool)
    for t, r0 in enumerate((0, NA_QROWS, rows - NA_QROWS)):
        ws = int(np.clip(r0 - NA_KH // 2, 0, rows - NA_KROWS))
        r = r0 + np.arange(NA_QROWS)
        rs = np.clip(r - NA_KH // 2, 0, rows - NA_KH)
        kr = ws + np.arange(NA_KROWS)
        valid_r[t] = (kr[None, :] >= rs[:, None]) & (kr[None, :] < rs[:, None] + NA_KH)
        dr = kr[None, :] - r[:, None] + (NA_KH - 1)
        onehot_r[t] = (dr[..., None] == np.arange(n_dr)) & valid_r[t][..., None]
    exact = lax.Precision.HIGHEST
    colexp = jnp.einsum('lhad,dqc->lhaqc', rpb, jnp.asarray(onehot_c), precision=exact)
    full = jnp.einsum('trka,lhaqc->lhtrqkc', jnp.asarray(onehot_r), colexp, precision=exact)
    mask = (np.where(valid_r, 0.0, NEG_BIG)[:, :, None, :, None]
            + np.where(valid_c, 0.0, NEG_BIG)[None, None, :, None, :]).astype(np.float32)
    full = (full + jnp.asarray(mask)[None, None]) * LOG2E
    L, H = rpb.shape[:2]
    return full.reshape(L, H, 3, NA_QROWS * GRID_W, NA_KROWS * GRID_W)


def _outproj_kernel(h_ref, a_ref, b_ref, c_ref, w_ref, o_ref):
    na = a_ref.shape[1]
    nb = b_ref.shape[1]
    acc = jnp.dot(a_ref[...], w_ref[0:na, :].astype(BF16), preferred_element_type=F32)
    acc += jnp.dot(b_ref[...], w_ref[na:na + nb, :].astype(BF16), preferred_element_type=F32)
    acc += jnp.dot(c_ref[...], w_ref[na + nb:, :].astype(BF16), preferred_element_type=F32)
    o_ref[...] = h_ref[...] + acc


def _outproj(h, oa, ob, oc, w, layer):
    T = h.shape[0]
    return pl.pallas_call(
        _outproj_kernel,
        grid=(T // OUT_TM, D_MODEL // OUT_TN),
        in_specs=[
            pl.BlockSpec((OUT_TM, OUT_TN), lambda i, j: (i, j)),
            pl.BlockSpec((OUT_TM, oa.shape[1]), lambda i, j: (i, 0)),
            pl.BlockSpec((OUT_TM, ob.shape[1]), lambda i, j: (i, 0)),
            pl.BlockSpec((OUT_TM, oc.shape[1]), lambda i, j: (i, 0)),
            pl.BlockSpec((None, D_MODEL, OUT_TN), lambda i, j: (layer, 0, j)),
        ],
        out_specs=pl.BlockSpec((OUT_TM, OUT_TN), lambda i, j: (i, j)),
        out_shape=jax.ShapeDtypeStruct((T, D_MODEL), F32),
        compiler_params=_params("parallel", "arbitrary"),
        name="outproj",
    )(h, oa, ob, oc, w)


def _ple_kernel(h_ref, g_ref, wg_ref, p_ref, wp_ref, gf_ref, o_ref, wg_bf_ref, wp_bf_ref, *, final):
    @pl.when(pl.program_id(0) == 0)
    def _():
        wg_bf_ref[...] = wg_ref[...].astype(BF16)
        wp_bf_ref[...] = wp_ref[...].astype(BF16)

    x = h_ref[...]
    xn = _rms_rows(x, g_ref[...]).astype(BF16)
    gate = jax.nn.sigmoid(jnp.dot(xn, wg_bf_ref[...], preferred_element_type=F32))
    emb = jnp.dot(p_ref[...].astype(BF16), wp_bf_ref[...], preferred_element_type=F32)
    y = x + gate * emb
    if final:
        y = _rms_rows(y, gf_ref[...])
    o_ref[...] = y


def _ple(h, g, wg, p, wp, g_final, layer, final):
    T = h.shape[0]
    resident = pl.Buffered(1)
    return pl.pallas_call(
        functools.partial(_ple_kernel, final=final),
        grid=(T // PLE_TM,),
        in_specs=[
            pl.BlockSpec((PLE_TM, D_MODEL), lambda i: (i, 0)),
            pl.BlockSpec((1, D_MODEL), lambda i: (0, 0)),
            pl.BlockSpec((None, D_MODEL, D_MODEL), lambda i: (layer, 0, 0), pipeline_mode=resident),
            pl.BlockSpec((None, PLE_TM, PLE_DIM), lambda i: (layer, i, 0)),
            pl.BlockSpec((None, PLE_DIM, D_MODEL), lambda i: (layer, 0, 0), pipeline_mode=resident),
            pl.BlockSpec((1, D_MODEL), lambda i: (0, 0)),
        ],
        out_specs=pl.BlockSpec((PLE_TM, D_MODEL), lambda i: (i, 0)),
        out_shape=jax.ShapeDtypeStruct((T, D_MODEL), F32),
        scratch_shapes=[pltpu.VMEM((D_MODEL, D_MODEL), BF16), pltpu.VMEM((PLE_DIM, D_MODEL), BF16)],
        compiler_params=_params("arbitrary"),
        name="ple",
    )(h, g.reshape(1, D_MODEL), wg, p, wp, g_final.reshape(1, D_MODEL))


def _rope_tables(S):
    t = jnp.arange(S)
    inv_ax = jnp.power(ROPE_THETA, -jnp.arange(0, AXIAL_DIM, 2, dtype=F32) / AXIAL_DIM)
    ang_b = jnp.concatenate([(t // GRID_W).astype(F32)[:, None] * inv_ax,
                             (t % GRID_W).astype(F32)[:, None] * inv_ax], axis=-1)
    cos_b, sin_b = jnp.cos(ang_b), jnp.sin(ang_b)
    cb = jnp.concatenate([cos_b, cos_b], axis=-1)
    sb = jnp.concatenate([-sin_b, sin_b], axis=-1)
    inv_1d = jnp.power(ROPE_THETA, -jnp.arange(0, DIFF_QK_DIM, 2, dtype=F32) / DIFF_QK_DIM)
    ang_c = t.astype(F32)[:, None] * inv_1d
    cos_c, sin_c = jnp.cos(ang_c), jnp.sin(ang_c)
    zero = jnp.zeros_like(sin_c)
    cc = jnp.concatenate([cos_c] * 4, axis=-1)
    sca = jnp.concatenate([-sin_c, zero, -sin_c, zero], axis=-1)
    scb = jnp.concatenate([zero, sin_c, zero, sin_c], axis=-1)
    return cb, sb, cc, sca, scb


def kernel(x, p, g_ffn1, w1_gate, w1_up, w1_down, g_mix, w_in, na_rpb, gqa_q_gain, gqa_k_gain,
           diff_lambda, diff_subln_gain, w_out, g_ffn2, w2_gate, w2_up, w2_down, g_ple,
           w_ple_gate, w_ple_proj, g_final):
    B, S, D = x.shape
    depth = w_in.shape[0]
    T = B * S
    rows = S // GRID_W
    tables = _rope_tables(S)
    bias = _na_bias_tables(na_rpb, rows)
    p_flat = p.reshape(depth, T, PLE_DIM)
    h = x.reshape(T, D)
    for i in range(depth):
        h = _ffn(h, g_ffn1[i], w1_gate, w1_up, w1_down, i)
        proj = _proj(h, g_mix[i], w_in, gqa_q_gain[i], gqa_k_gain[i], tables, S, i)
        out_a = _na(proj, bias, B, S, i)
        out_b = _gqa(proj, B, S)
        out_c = _diff(proj, diff_lambda[i], diff_subln_gain[i], i, B, S)
        h = _outproj(h, out_a, out_b, out_c, w_out, i)
        h = _ffn(h, g_ffn2[i], w2_gate, w2_up, w2_down, i)
        h = _ple(h, g_ple[i], w_ple_gate, p_flat, w_ple_proj, g_final, i, final=(i == depth - 1))
    return h.reshape(B, S, D)
```

```python
import functools
import math

import numpy as np
import jax
import jax.numpy as jnp
from jax import lax
from jax.experimental import pallas as pl
from jax.experimental.pallas import tpu as pltpu

F32 = jnp.float32
BF16 = jnp.bfloat16

D_MODEL = 2048
EPS = 1e-6
GRID_W = 64
HEAD_DIM = 128
NA_HEADS = 4
NA_KH = 8
NA_KW = 16
GQA_HEADS = 8
GQA_KV = 2
GQA_GROUP = GQA_HEADS // GQA_KV
AXIAL_DIM = HEAD_DIM // 2
DIFF_HEADS = 4
DIFF_QK_DIM = HEAD_DIM // 2
IN_COLS = 4608
D_FF = 5632
PLE_DIM = 256
ROPE_THETA = 10000.0

NEG_BIG = -1e30
LOG2E = math.log2(math.e)
VMEM_LIMIT_BYTES = 56 * 1024 * 1024

FFN_TM = 1024
FFN_TF = 256
PROJ_TM = 1024
PROJ_TN = 512
OUT_TM = 1024
OUT_TN = 1024
PLE_TM = 512
ATT_RB = 512
ATT_KC = 512
NA_QROWS = 8
NA_KROWS = 16


def _params(*sem):
    return pltpu.CompilerParams(dimension_semantics=sem, vmem_limit_bytes=VMEM_LIMIT_BYTES)


def _rms_rows(x, g):
    return x * lax.rsqrt(jnp.mean(x * x, axis=-1, keepdims=True) + EPS) * g


def _ffn_kernel(h_ref, g_ref, wg_ref, wu_ref, wd_ref, o_ref, xn_ref):
    @pl.when(pl.program_id(1) == 0)
    def _():
        x = h_ref[...]
        xn_ref[...] = _rms_rows(x, g_ref[...]).astype(BF16)
        o_ref[...] = x

    xn = xn_ref[...]
    a = jnp.dot(xn, wg_ref[...].astype(BF16), preferred_element_type=F32)
    u = jnp.dot(xn, wu_ref[...].astype(BF16), preferred_element_type=F32)
    act = (a * jax.nn.sigmoid(a) * (0.5 * u)).astype(BF16)
    o_ref[...] += jnp.dot(act, wd_ref[...].astype(BF16), preferred_element_type=F32)


def _ffn(h, g, wg, wu, wd, layer):
    T = h.shape[0]
    return pl.pallas_call(
        _ffn_kernel,
        grid=(T // FFN_TM, D_FF // FFN_TF),
        in_specs=[
            pl.BlockSpec((FFN_TM, D_MODEL), lambda i, j: (i, 0), pipeline_mode=pl.Buffered(1)),
            pl.BlockSpec((1, D_MODEL), lambda i, j: (0, 0)),
            pl.BlockSpec((None, D_MODEL, FFN_TF), lambda i, j: (layer, 0, j)),
            pl.BlockSpec((None, D_MODEL, FFN_TF), lambda i, j: (layer, 0, j)),
            pl.BlockSpec((None, FFN_TF, D_MODEL), lambda i, j: (layer, j, 0)),
        ],
        out_specs=pl.BlockSpec((FFN_TM, D_MODEL), lambda i, j: (i, 0)),
        out_shape=jax.ShapeDtypeStruct((T, D_MODEL), F32),
        scratch_shapes=[pltpu.VMEM((FFN_TM, D_MODEL), BF16)],
        compiler_params=_params("parallel", "arbitrary"),
        name="ffn",
    )(h, g.reshape(1, D_MODEL), wg, wu, wd)


def _proj_kernel(h_ref, g_ref, w_ref, qg_ref, kg_ref, cb_ref, sb_ref, cc_ref, sca_ref, scb_ref,
                 o_ref, xn_ref):
    j = pl.program_id(1)

    @pl.when(j == 0)
    def _():
        xn_ref[...] = _rms_rows(h_ref[...], g_ref[...]).astype(BF16)

    y = jnp.dot(xn_ref[...], w_ref[...].astype(BF16), preferred_element_type=F32)
    heads = [y[:, k * HEAD_DIM:(k + 1) * HEAD_DIM] for k in range(PROJ_TN // HEAD_DIM)]

    def put(k, val):
        o_ref[:, k * HEAD_DIM:(k + 1) * HEAD_DIM] = val.astype(BF16)

    def rope_axial(x):
        return x * cb_ref[...] + pltpu.roll(x, HEAD_DIM // 2, 1) * sb_ref[...]

    def rope_pair(x):
        return (x * cc_ref[...] + pltpu.roll(x, HEAD_DIM - DIFF_QK_DIM // 2, 1) * sca_ref[...]
                + pltpu.roll(x, DIFF_QK_DIM // 2, 1) * scb_ref[...])

    @pl.when(j == 0)
    def _():
        for k in range(4):
            put(k, heads[k] * (HEAD_DIM ** -0.5 * LOG2E))

    @pl.when((j == 1) | (j == 2) | (j == 8))
    def _():
        for k in range(4):
            put(k, heads[k])

    @pl.when((j == 3) | (j == 4))
    def _():
        for k in range(4):
            put(k, rope_axial(_rms_rows(heads[k], qg_ref[...])) * (HEAD_DIM ** -0.5 * LOG2E))

    @pl.when(j == 5)
    def _():
        for k in range(2):
            put(k, rope_axial(_rms_rows(heads[k], kg_ref[...])))
        for k in range(2, 4):
            put(k, heads[k])

    @pl.when(j == 6)
    def _():
        for k in range(4):
            put(k, rope_pair(heads[k]) * (DIFF_QK_DIM ** -0.5 * LOG2E))

    @pl.when(j == 7)
    def _():
        for k in range(4):
            put(k, rope_pair(heads[k]))


def _proj(h, g, w, qg, kg, tables, S, layer):
    T = h.shape[0]
    nst = S // PROJ_TM
    tab_spec = pl.BlockSpec((PROJ_TM, HEAD_DIM), lambda i, j: (i % nst, 0))
    vec_spec = pl.BlockSpec((1, HEAD_DIM), lambda i, j: (0, 0))
    return pl.pallas_call(
        _proj_kernel,
        grid=(T // PROJ_TM, IN_COLS // PROJ_TN),
        in_specs=[
            pl.BlockSpec((PROJ_TM, D_MODEL), lambda i, j: (i, 0)),
            pl.BlockSpec((1, D_MODEL), lambda i, j: (0, 0)),
            pl.BlockSpec((None, D_MODEL, PROJ_TN), lambda i, j: (layer, 0, j)),
            vec_spec, vec_spec, tab_spec, tab_spec, tab_spec, tab_spec, tab_spec,
        ],
        out_specs=pl.BlockSpec((PROJ_TM, PROJ_TN), lambda i, j: (i, j)),
        out_shape=jax.ShapeDtypeStruct((T, IN_COLS), BF16),
        scratch_shapes=[pltpu.VMEM((PROJ_TM, D_MODEL), BF16)],
        compiler_params=_params("parallel", "arbitrary"),
        name="proj",
    )(h, g.reshape(1, D_MODEL), w, qg.reshape(1, HEAD_DIM), kg.reshape(1, HEAD_DIM), *tables)


def _transpose_values(v_ref, vt_ref):
    S = v_ref.shape[0]
    step = 512
    for c in range(S // step):
        blk = v_ref[c * step:(c + 1) * step, :].astype(F32)
        vt_ref[:, c * step:(c + 1) * step] = blk.T.astype(BF16)


def _attention_rows(q_ref, k_ref, vt_ref, s_refs, emit):
    npair = q_ref.shape[0] // (2 * ATT_RB)
    S = k_ref.shape[0]

    def scores(r, slot):
        off = pl.multiple_of(r * ATT_RB, ATT_RB)
        st = lax.dot_general(k_ref[...], q_ref[pl.ds(off, ATT_RB), :], (((1,), (1,)), ((), ())),
                             preferred_element_type=F32)
        s_refs[slot][...] = st
        return jnp.max(st, axis=0, keepdims=True)

    def consume(slot, m):
        acc = jnp.zeros((HEAD_DIM, ATT_RB), F32)
        l = jnp.zeros((1, ATT_RB), F32)
        for c in range(S // ATT_KC):
            p = jnp.exp2(s_refs[slot][c * ATT_KC:(c + 1) * ATT_KC, :] - m)
            l = l + jnp.sum(p, axis=0, keepdims=True)
            acc = acc + jnp.dot(vt_ref[:, c * ATT_KC:(c + 1) * ATT_KC], p.astype(BF16),
                                preferred_element_type=F32)
        return (acc / l).T

    def body(i, m_even):
        m_odd = scores(2 * i + 1, 1)
        o_even = consume(0, m_even)
        m_next = scores(2 * i + 2, 0)
        o_odd = consume(1, m_odd)
        emit(i, o_even, o_odd)
        return m_next

    m_even = lax.fori_loop(0, npair - 1, body, scores(0, 0))
    m_odd = scores(2 * npair - 1, 1)
    o_even = consume(0, m_even)
    o_odd = consume(1, m_odd)
    emit(npair - 1, o_even, o_odd)


def _gqa_kernel(q_ref, k_ref, v_ref, o_ref, vt_ref, s0_ref, s1_ref, qs_ref, os_ref):
    S = k_ref.shape[0]
    _transpose_values(v_ref, vt_ref)
    for g in range(GQA_GROUP):
        qs_ref[g * S:(g + 1) * S, :] = q_ref[:, g * HEAD_DIM:(g + 1) * HEAD_DIM]

    def emit(i, o_even, o_odd):
        off = pl.multiple_of(i * 2 * ATT_RB, 2 * ATT_RB)
        os_ref[pl.ds(off, ATT_RB), :] = o_even.astype(os_ref.dtype)
        os_ref[pl.ds(off + ATT_RB, ATT_RB), :] = o_odd.astype(os_ref.dtype)

    _attention_rows(qs_ref, k_ref, vt_ref, (s0_ref, s1_ref), emit)
    for g in range(GQA_GROUP):
        o_ref[:, g * HEAD_DIM:(g + 1) * HEAD_DIM] = os_ref[g * S:(g + 1) * S, :]


def _gqa(proj, B, S):
    T = proj.shape[0]
    gw = GQA_GROUP * HEAD_DIM
    q0 = 1536 // gw
    k0 = (1536 + GQA_HEADS * HEAD_DIM) // HEAD_DIM
    v0 = k0 + GQA_KV
    return pl.pallas_call(
        _gqa_kernel,
        grid=(B, GQA_KV),
        in_specs=[
            pl.BlockSpec((S, gw), lambda b, kh: (b, q0 + kh)),
            pl.BlockSpec((S, HEAD_DIM), lambda b, kh: (b, k0 + kh)),
            pl.BlockSpec((S, HEAD_DIM), lambda b, kh: (b, v0 + kh)),
        ],
        out_specs=pl.BlockSpec((S, gw), lambda b, kh: (b, kh)),
        out_shape=jax.ShapeDtypeStruct((T, GQA_HEADS * HEAD_DIM), BF16),
        scratch_shapes=[pltpu.VMEM((HEAD_DIM, S), BF16),
                        pltpu.VMEM((S, ATT_RB), F32), pltpu.VMEM((S, ATT_RB), F32),
                        pltpu.VMEM((GQA_GROUP * S, HEAD_DIM), BF16),
                        pltpu.VMEM((GQA_GROUP * S, HEAD_DIM), BF16)],
        compiler_params=_params("parallel", "parallel"),
        name="gqa",
    )(proj, proj, proj)


def _diff_kernel(lam_ref, gain_ref, q_ref, k_ref, v_ref, o_ref, vt_ref, s0_ref, s1_ref, qc_ref, *, lam_init):
    _transpose_values(v_ref, vt_ref)

    S = q_ref.shape[0]
    lane = lax.broadcasted_iota(jnp.int32, (ATT_RB, HEAD_DIM), 1)
    first = lane < DIFF_QK_DIM
    for r in range(S // ATT_RB):
        q = q_ref[r * ATT_RB:(r + 1) * ATT_RB, :]
        zero = jnp.zeros_like(q)
        qc_ref[2 * r * ATT_RB:(2 * r + 1) * ATT_RB, :] = jnp.where(first, q, zero)
        qc_ref[(2 * r + 1) * ATT_RB:(2 * r + 2) * ATT_RB, :] = jnp.where(first, zero, q)

    lp = lam_ref[...]
    lam = (jnp.exp(jnp.sum(lp[0:1] * lp[1:2], axis=-1, keepdims=True))
           - jnp.exp(jnp.sum(lp[2:3] * lp[3:4], axis=-1, keepdims=True)) + lam_init)

    def emit(i, o1, o2):
        off = pl.multiple_of(i * ATT_RB, ATT_RB)
        o = o1 - lam * o2
        o = _rms_rows(o, gain_ref[...]) * (1.0 - lam_init)
        o_ref[pl.ds(off, ATT_RB), :] = o.astype(o_ref.dtype)

    _attention_rows(qc_ref, k_ref, vt_ref, (s0_ref, s1_ref), emit)


def _diff(proj, lam_params, gain, layer, B, S):
    T = proj.shape[0]
    q0 = 3072 // HEAD_DIM
    k0 = q0 + DIFF_HEADS
    v0 = k0 + DIFF_HEADS
    lam_init = 0.8 - 0.6 * math.exp(-0.3 * layer)
    return pl.pallas_call(
        functools.partial(_diff_kernel, lam_init=lam_init),
        grid=(B, DIFF_HEADS),
        in_specs=[
            pl.BlockSpec((4, DIFF_QK_DIM), lambda b, h: (0, 0)),
            pl.BlockSpec((1, HEAD_DIM), lambda b, h: (0, 0)),
            pl.BlockSpec((S, HEAD_DIM), lambda b, h: (b, q0 + h)),
            pl.BlockSpec((S, HEAD_DIM), lambda b, h: (b, k0 + h)),
            pl.BlockSpec((S, HEAD_DIM), lambda b, h: (b, v0 + h)),
        ],
        out_specs=pl.BlockSpec((S, HEAD_DIM), lambda b, h: (b, h)),
        out_shape=jax.ShapeDtypeStruct((T, DIFF_HEADS * HEAD_DIM), BF16),
        scratch_shapes=[pltpu.VMEM((HEAD_DIM, S), BF16),
                        pltpu.VMEM((S, ATT_RB), F32), pltpu.VMEM((S, ATT_RB), F32),
                        pltpu.VMEM((2 * S, HEAD_DIM), BF16)],
        compiler_params=_params("parallel", "parallel"),
        name="diff",
    )(lam_params, gain.reshape(1, HEAD_DIM), proj, proj, proj)


def _na_window(blk, rows):
    r0 = blk * NA_QROWS
    return r0, int(np.clip(r0 - NA_KH // 2, 0, rows - NA_KROWS))


def _na_slab_index(r0, ws, k, j, rows):
    n_dr = 2 * NA_KH - 1
    kr = ws + k
    a, valid = [], []
    for r in (r0 + 2 * j, r0 + 2 * j + 1):
        rs = int(np.clip(r - NA_KH // 2, 0, rows - NA_KH))
        valid.append(rs <= kr < rs + NA_KH)
        a.append(kr - r + (NA_KH - 1))
    if valid[0] and valid[1]:
        return a[0] - 1
    if valid[0]:
        return (n_dr - 1) + a[0]
    if valid[1]:
        return (n_dr - 1) + n_dr + a[1]
    return None


def _na_kernel(q_ref, k_ref, v_ref, slab_ref, o_ref, vt_ref, *, rows):
    _transpose_values(v_ref, vt_ref)
    tq = NA_QROWS * GRID_W
    tk = NA_KROWS * GRID_W
    masked = jnp.full((GRID_W, 2 * GRID_W), NEG_BIG, F32)
    for blk in range(rows // NA_QROWS):
        r0, ws = _na_window(blk, rows)
        bias_rows = []
        for k in range(NA_KROWS):
            slabs = []
            for j in range(NA_QROWS // 2):
                idx = _na_slab_index(r0, ws, k, j, rows)
                slabs.append(masked if idx is None else slab_ref[idx])
            bias_rows.append(jnp.concatenate(slabs, axis=1))
        bias = jnp.concatenate(bias_rows, axis=0)
        q = q_ref[r0 * GRID_W:r0 * GRID_W + tq, :]
        kw = k_ref[ws * GRID_W:ws * GRID_W + tk, :]
        st = lax.dot_general(kw, q, (((1,), (1,)), ((), ())), preferred_element_type=F32) + bias
        m = jnp.max(st, axis=0, keepdims=True)
        p = jnp.exp2(st - m)
        l = jnp.sum(p, axis=0, keepdims=True)
        ot = jnp.dot(vt_ref[:, ws * GRID_W:ws * GRID_W + tk], p.astype(BF16), preferred_element_type=F32)
        o_ref[r0 * GRID_W:r0 * GRID_W + tq, :] = (ot / l).T.astype(o_ref.dtype)


def _na(proj, slabs, B, S, layer):
    T = proj.shape[0]
    rows = S // GRID_W
    n_slab = slabs.shape[2]
    return pl.pallas_call(
        functools.partial(_na_kernel, rows=rows),
        grid=(B, NA_HEADS),
        in_specs=[
            pl.BlockSpec((S, HEAD_DIM), lambda b, h: (b, h)),
            pl.BlockSpec((S, HEAD_DIM), lambda b, h: (b, NA_HEADS + h)),
            pl.BlockSpec((S, HEAD_DIM), lambda b, h: (b, 2 * NA_HEADS + h)),
            pl.BlockSpec((None, None, n_slab, GRID_W, 2 * GRID_W), lambda b, h: (layer, h, 0, 0, 0)),
        ],
        out_specs=pl.BlockSpec((S, HEAD_DIM), lambda b, h: (b, h)),
        out_shape=jax.ShapeDtypeStruct((T, NA_HEADS * HEAD_DIM), BF16),
        scratch_shapes=[pltpu.VMEM((HEAD_DIM, S), BF16)],
        compiler_params=_params("parallel", "parallel"),
        name="na",
    )(proj, proj, proj, slabs)


def _na_bias_slabs(rpb):
    n_dc = 2 * NA_KW - 1
    cq = np.arange(GRID_W)
    cs = np.clip(cq - NA_KW // 2, 0, GRID_W - NA_KW)
    ck = np.arange(GRID_W)
    valid_c = (ck[:, None] >= cs[None, :]) & (ck[:, None] < cs[None, :] + NA_KW)
    dc = ck[:, None] - cq[None, :] + (NA_KW - 1)
    onehot = ((dc[None] == np.arange(n_dc)[:, None, None]) & valid_c[None]).astype(np.float32)
    t = jnp.einsum('lhad,dkq->lhakq', rpb, jnp.asarray(onehot), precision=lax.Precision.HIGHEST)
    t = (t + jnp.asarray(np.where(valid_c, 0.0, NEG_BIG).astype(np.float32))) * LOG2E
    masked = jnp.full_like(t, NEG_BIG)
    both = jnp.concatenate([t[:, :, 1:], t[:, :, :-1]], axis=-1)
    left = jnp.concatenate([t, masked], axis=-1)
    right = jnp.concatenate([masked, t], axis=-1)
    return jnp.concatenate([both, left, right], axis=2)


def _outproj_kernel(h_ref, a_ref, b_ref, c_ref, w_ref, o_ref):
    na = a_ref.shape[1]
    nb = b_ref.shape[1]
    acc = jnp.dot(a_ref[...], w_ref[0:na, :].astype(BF16), preferred_element_type=F32)
    acc += jnp.dot(b_ref[...], w_ref[na:na + nb, :].astype(BF16), preferred_element_type=F32)
    acc += jnp.dot(c_ref[...], w_ref[na + nb:, :].astype(BF16), preferred_element_type=F32)
    o_ref[...] = h_ref[...] + acc


def _outproj(h, oa, ob, oc, w, layer):
    T = h.shape[0]
    return pl.pallas_call(
        _outproj_kernel,
        grid=(T // OUT_TM, D_MODEL // OUT_TN),
        in_specs=[
            pl.BlockSpec((OUT_TM, OUT_TN), lambda i, j: (i, j)),
            pl.BlockSpec((OUT_TM, oa.shape[1]), lambda i, j: (i, 0)),
            pl.BlockSpec((OUT_TM, ob.shape[1]), lambda i, j: (i, 0)),
            pl.BlockSpec((OUT_TM, oc.shape[1]), lambda i, j: (i, 0)),
            pl.BlockSpec((None, D_MODEL, OUT_TN), lambda i, j: (layer, 0, j)),
        ],
        out_specs=pl.BlockSpec((OUT_TM, OUT_TN), lambda i, j: (i, j)),
        out_shape=jax.ShapeDtypeStruct((T, D_MODEL), F32),
        compiler_params=_params("parallel", "arbitrary"),
        name="outproj",
    )(h, oa, ob, oc, w)


def _ple_kernel(h_ref, g_ref, wg_ref, p_ref, wp_ref, gf_ref, o_ref, wg_bf_ref, wp_bf_ref, *, final):
    @pl.when(pl.program_id(0) == 0)
    def _():
        wg_bf_ref[...] = wg_ref[...].astype(BF16)
        wp_bf_ref[...] = wp_ref[...].astype(BF16)

    x = h_ref[...]
    xn = _rms_rows(x, g_ref[...]).astype(BF16)
    gate = jax.nn.sigmoid(jnp.dot(xn, wg_bf_ref[...], preferred_element_type=F32))
    emb = jnp.dot(p_ref[...].astype(BF16), wp_bf_ref[...], preferred_element_type=F32)
    y = x + gate * emb
    if final:
        y = _rms_rows(y, gf_ref[...])
    o_ref[...] = y


def _ple(h, g, wg, p, wp, g_final, layer, final):
    T = h.shape[0]
    resident = pl.Buffered(1)
    return pl.pallas_call(
        functools.partial(_ple_kernel, final=final),
        grid=(T // PLE_TM,),
        in_specs=[
            pl.BlockSpec((PLE_TM, D_MODEL), lambda i: (i, 0)),
            pl.BlockSpec((1, D_MODEL), lambda i: (0, 0)),
            pl.BlockSpec((None, D_MODEL, D_MODEL), lambda i: (layer, 0, 0), pipeline_mode=resident),
            pl.BlockSpec((None, PLE_TM, PLE_DIM), lambda i: (layer, i, 0)),
            pl.BlockSpec((None, PLE_DIM, D_MODEL), lambda i: (layer, 0, 0), pipeline_mode=resident),
            pl.BlockSpec((1, D_MODEL), lambda i: (0, 0)),
        ],
        out_specs=pl.BlockSpec((PLE_TM, D_MODEL), lambda i: (i, 0)),
        out_shape=jax.ShapeDtypeStruct((T, D_MODEL), F32),
        scratch_shapes=[pltpu.VMEM((D_MODEL, D_MODEL), BF16), pltpu.VMEM((PLE_DIM, D_MODEL), BF16)],
        compiler_params=_params("arbitrary"),
        name="ple",
    )(h, g.reshape(1, D_MODEL), wg, p, wp, g_final.reshape(1, D_MODEL))


def _rope_tables(S):
    t = jnp.arange(S)
    inv_ax = jnp.power(ROPE_THETA, -jnp.arange(0, AXIAL_DIM, 2, dtype=F32) / AXIAL_DIM)
    ang_b = jnp.concatenate([(t // GRID_W).astype(F32)[:, None] * inv_ax,
                             (t % GRID_W).astype(F32)[:, None] * inv_ax], axis=-1)
    cos_b, sin_b = jnp.cos(ang_b), jnp.sin(ang_b)
    cb = jnp.concatenate([cos_b, cos_b], axis=-1)
    sb = jnp.concatenate([-sin_b, sin_b], axis=-1)
    inv_1d = jnp.power(ROPE_THETA, -jnp.arange(0, DIFF_QK_DIM, 2, dtype=F32) / DIFF_QK_DIM)
    ang_c = t.astype(F32)[:, None] * inv_1d
    cos_c, sin_c = jnp.cos(ang_c), jnp.sin(ang_c)
    zero = jnp.zeros_like(sin_c)
    cc = jnp.concatenate([cos_c] * 4, axis=-1)
    sca = jnp.concatenate([-sin_c, zero, -sin_c, zero], axis=-1)
    scb = jnp.concatenate([zero, sin_c, zero, sin_c], axis=-1)
    return cb, sb, cc, sca, scb


def kernel(x, p, g_ffn1, w1_gate, w1_up, w1_down, g_mix, w_in, na_rpb, gqa_q_gain, gqa_k_gain,
           diff_lambda, diff_subln_gain, w_out, g_ffn2, w2_gate, w2_up, w2_down, g_ple,
           w_ple_gate, w_ple_proj, g_final):
    B, S, D = x.shape
    depth = w_in.shape[0]
    T = B * S
    tables = _rope_tables(S)
    slabs = _na_bias_slabs(na_rpb)
    p_flat = p.reshape(depth, T, PLE_DIM)
    h = x.reshape(T, D)
    for i in range(depth):
        h = _ffn(h, g_ffn1[i], w1_gate, w1_up, w1_down, i)
        proj = _proj(h, g_mix[i], w_in, gqa_q_gain[i], gqa_k_gain[i], tables, S, i)
        out_a = _na(proj, slabs, B, S, i)
        out_b = _gqa(proj, B, S)
        out_c = _diff(proj, diff_lambda[i], diff_subln_gain[i], i, B, S)
        h = _outproj(h, out_a, out_b, out_c, w_out, i)
        h = _ffn(h, g_ffn2[i], w2_gate, w2_up, w2_down, i)
        h = _ple(h, g_ple[i], w_ple_gate, p_flat, w_ple_proj, g_final, i, final=(i == depth - 1))
    return h.reshape(B, S, D)
```

```python
import functools
import math

import numpy as np
import jax
import jax.numpy as jnp
from jax import lax
from jax.experimental import pallas as pl
from jax.experimental.pallas import tpu as pltpu

F32 = jnp.float32
BF16 = jnp.bfloat16

D_MODEL = 2048
EPS = 1e-6
GRID_W = 64
HEAD_DIM = 128
NA_HEADS = 4
NA_KH = 8
NA_KW = 16
GQA_HEADS = 8
GQA_KV = 2
GQA_GROUP = GQA_HEADS // GQA_KV
AXIAL_DIM = HEAD_DIM // 2
DIFF_HEADS = 4
DIFF_QK_DIM = HEAD_DIM // 2
IN_COLS = 4608
D_FF = 5632
PLE_DIM = 256
ROPE_THETA = 10000.0

NEG_BIG = -1e30
LOG2E = math.log2(math.e)
VMEM_LIMIT_BYTES = 56 * 1024 * 1024

FFN_TM = 1024
FFN_TF = 256
PROJ_TM = 1024
PROJ_TN = 512
OUT_TM = 512
PLE_TM = 512
ATT_RB = 512
ATT_KC = 512
NA_QROWS = 8
NA_KROWS = 16


def _params(*sem):
    return pltpu.CompilerParams(dimension_semantics=sem, vmem_limit_bytes=VMEM_LIMIT_BYTES)


def _rms_rows(x, g):
    return x * lax.rsqrt(jnp.mean(x * x, axis=-1, keepdims=True) + EPS) * g


def _ffn_kernel(h_ref, g_ref, wg_ref, wu_ref, wd_ref, o_ref, xn_ref):
    @pl.when(pl.program_id(1) == 0)
    def _():
        x = h_ref[...]
        xn_ref[...] = _rms_rows(x, g_ref[...]).astype(BF16)
        o_ref[...] = x

    xn = xn_ref[...]
    a = jnp.dot(xn, wg_ref[...].astype(BF16), preferred_element_type=F32)
    u = jnp.dot(xn, wu_ref[...].astype(BF16), preferred_element_type=F32)
    act = (a * jax.nn.sigmoid(a) * (0.5 * u)).astype(BF16)
    o_ref[...] += jnp.dot(act, wd_ref[...].astype(BF16), preferred_element_type=F32)


def _ffn(h, g, wg, wu, wd, layer):
    T = h.shape[0]
    return pl.pallas_call(
        _ffn_kernel,
        grid=(T // FFN_TM, D_FF // FFN_TF),
        in_specs=[
            pl.BlockSpec((FFN_TM, D_MODEL), lambda i, j: (i, 0), pipeline_mode=pl.Buffered(1)),
            pl.BlockSpec((1, D_MODEL), lambda i, j: (0, 0)),
            pl.BlockSpec((None, D_MODEL, FFN_TF), lambda i, j: (layer, 0, j)),
            pl.BlockSpec((None, D_MODEL, FFN_TF), lambda i, j: (layer, 0, j)),
            pl.BlockSpec((None, FFN_TF, D_MODEL), lambda i, j: (layer, j, 0)),
        ],
        out_specs=pl.BlockSpec((FFN_TM, D_MODEL), lambda i, j: (i, 0)),
        out_shape=jax.ShapeDtypeStruct((T, D_MODEL), F32),
        scratch_shapes=[pltpu.VMEM((FFN_TM, D_MODEL), BF16)],
        compiler_params=_params("parallel", "arbitrary"),
        name="ffn",
    )(h, g.reshape(1, D_MODEL), wg, wu, wd)


PLAIN_TILES = 4
AXIAL_TILE0, AXIAL_TILES = 3, 3
PAIR_TILE0, PAIR_TILES = 6, 2
LAST_TILE = IN_COLS // PROJ_TN - 1


def _proj_plain_kernel(h_ref, g_ref, w_ref, o_ref, xn_ref):
    j = pl.program_id(1)

    @pl.when(j == 0)
    def _():
        xn_ref[...] = _rms_rows(h_ref[...], g_ref[...]).astype(BF16)

    y = jnp.dot(xn_ref[...], w_ref[...].astype(BF16), preferred_element_type=F32)
    scale = jnp.where(j == 0, HEAD_DIM ** -0.5 * LOG2E, 1.0)
    o_ref[...] = (y * scale).astype(BF16)


def _proj_plain(h, g, w, layer):
    T = h.shape[0]
    return pl.pallas_call(
        _proj_plain_kernel,
        grid=(T // PROJ_TM, PLAIN_TILES),
        in_specs=[
            pl.BlockSpec((PROJ_TM, D_MODEL), lambda i, j: (i, 0)),
            pl.BlockSpec((1, D_MODEL), lambda i, j: (0, 0)),
            pl.BlockSpec((None, D_MODEL, PROJ_TN),
                         lambda i, j: (layer, 0, jnp.where(j < PLAIN_TILES - 1, j, LAST_TILE))),
        ],
        out_specs=[pl.BlockSpec((PROJ_TM, PROJ_TN), lambda i, j: (i, j)),
                   pl.BlockSpec((PROJ_TM, D_MODEL), lambda i, j: (i, 0))],
        out_shape=[jax.ShapeDtypeStruct((T, PLAIN_TILES * PROJ_TN), BF16),
                   jax.ShapeDtypeStruct((T, D_MODEL), BF16)],
        compiler_params=_params("parallel", "arbitrary"),
        name="proj_plain",
    )(h, g.reshape(1, D_MODEL), w)


def _proj_axial_kernel(xn_ref, w_ref, qg_ref, kg_ref, cb_ref, sb_ref, o_ref):
    j = pl.program_id(1)
    y = jnp.dot(xn_ref[...], w_ref[...].astype(BF16), preferred_element_type=F32)
    heads = [y[:, k * HEAD_DIM:(k + 1) * HEAD_DIM] for k in range(PROJ_TN // HEAD_DIM)]

    def put(k, val):
        o_ref[:, k * HEAD_DIM:(k + 1) * HEAD_DIM] = val.astype(BF16)

    def norm_rope(x, gain):
        xg = _rms_rows(x, gain)
        return xg * cb_ref[...] + pltpu.roll(xg, HEAD_DIM // 2, 1) * sb_ref[...]

    @pl.when(j < AXIAL_TILES - 1)
    def _():
        for k in range(4):
            put(k, norm_rope(heads[k], qg_ref[...]))

    @pl.when(j == AXIAL_TILES - 1)
    def _():
        for k in range(GQA_KV):
            put(k, norm_rope(heads[k], kg_ref[...]))
        for k in range(GQA_KV, 4):
            put(k, heads[k])


def _proj_axial(xn, w, qg, kg, cb, sb, S, layer):
    T = xn.shape[0]
    nst = S // PROJ_TM
    tab_spec = pl.BlockSpec((PROJ_TM, HEAD_DIM), lambda i, j: (i % nst, 0))
    vec_spec = pl.BlockSpec((1, HEAD_DIM), lambda i, j: (0, 0))
    return pl.pallas_call(
        _proj_axial_kernel,
        grid=(T // PROJ_TM, AXIAL_TILES),
        in_specs=[
            pl.BlockSpec((PROJ_TM, D_MODEL), lambda i, j: (i, 0)),
            pl.BlockSpec((None, D_MODEL, PROJ_TN), lambda i, j: (layer, 0, AXIAL_TILE0 + j)),
            vec_spec, vec_spec, tab_spec, tab_spec,
        ],
        out_specs=pl.BlockSpec((PROJ_TM, PROJ_TN), lambda i, j: (i, j)),
        out_shape=jax.ShapeDtypeStruct((T, AXIAL_TILES * PROJ_TN), BF16),
        compiler_params=_params("parallel", "parallel"),
        name="proj_axial",
    )(xn, w, qg.reshape(1, HEAD_DIM), kg.reshape(1, HEAD_DIM), cb, sb)


def _proj_pair_kernel(xn_ref, w_ref, cc_ref, sca_ref, scb_ref, o_ref):
    y = jnp.dot(xn_ref[...], w_ref[...].astype(BF16), preferred_element_type=F32)
    scale = jnp.where(pl.program_id(1) == 0, DIFF_QK_DIM ** -0.5 * LOG2E, 1.0)
    for k in range(PROJ_TN // HEAD_DIM):
        x = y[:, k * HEAD_DIM:(k + 1) * HEAD_DIM]
        r = (x * cc_ref[...] + pltpu.roll(x, HEAD_DIM - DIFF_QK_DIM // 2, 1) * sca_ref[...]
             + pltpu.roll(x, DIFF_QK_DIM // 2, 1) * scb_ref[...])
        o_ref[:, k * HEAD_DIM:(k + 1) * HEAD_DIM] = (r * scale).astype(BF16)


def _proj_pair(xn, w, cc, sca, scb, S, layer):
    T = xn.shape[0]
    nst = S // PROJ_TM
    tab_spec = pl.BlockSpec((PROJ_TM, HEAD_DIM), lambda i, j: (i % nst, 0))
    return pl.pallas_call(
        _proj_pair_kernel,
        grid=(T // PROJ_TM, PAIR_TILES),
        in_specs=[
            pl.BlockSpec((PROJ_TM, D_MODEL), lambda i, j: (i, 0)),
            pl.BlockSpec((None, D_MODEL, PROJ_TN), lambda i, j: (layer, 0, PAIR_TILE0 + j)),
            tab_spec, tab_spec, tab_spec,
        ],
        out_specs=pl.BlockSpec((PROJ_TM, PROJ_TN), lambda i, j: (i, j)),
        out_shape=jax.ShapeDtypeStruct((T, PAIR_TILES * PROJ_TN), BF16),
        compiler_params=_params("parallel", "parallel"),
        name="proj_pair",
    )(xn, w, cc, sca, scb)


def _transpose_values(v_ref, vt_ref):
    S = v_ref.shape[0]
    step = 512
    for c in range(S // step):
        blk = v_ref[c * step:(c + 1) * step, :].astype(F32)
        vt_ref[:, c * step:(c + 1) * step] = blk.T.astype(BF16)


def _attention_rows(q_ref, k_ref, vt_ref, s_refs, emit):
    npair = q_ref.shape[0] // (2 * ATT_RB)
    S = k_ref.shape[0]

    def scores(r, slot):
        off = pl.multiple_of(r * ATT_RB, ATT_RB)
        st = lax.dot_general(k_ref[...], q_ref[pl.ds(off, ATT_RB), :], (((1,), (1,)), ((), ())),
                             preferred_element_type=F32)
        s_refs[slot][...] = st
        return jnp.max(st, axis=0, keepdims=True)

    def consume(slot, m):
        acc = jnp.zeros((HEAD_DIM, ATT_RB), F32)
        l = jnp.zeros((1, ATT_RB), F32)
        for c in range(S // ATT_KC):
            p = jnp.exp2(s_refs[slot][c * ATT_KC:(c + 1) * ATT_KC, :] - m)
            l = l + jnp.sum(p, axis=0, keepdims=True)
            acc = acc + jnp.dot(vt_ref[:, c * ATT_KC:(c + 1) * ATT_KC], p.astype(BF16),
                                preferred_element_type=F32)
        return (acc / l).T

    def body(i, m_even):
        m_odd = scores(2 * i + 1, 1)
        o_even = consume(0, m_even)
        m_next = scores(2 * i + 2, 0)
        o_odd = consume(1, m_odd)
        emit(i, o_even, o_odd)
        return m_next

    m_even = lax.fori_loop(0, npair - 1, body, scores(0, 0))
    m_odd = scores(2 * npair - 1, 1)
    o_even = consume(0, m_even)
    o_odd = consume(1, m_odd)
    emit(npair - 1, o_even, o_odd)


def _gqa_kernel(q_ref, k_ref, v_ref, o_ref, vt_ref, s0_ref, s1_ref, qs_ref, os_ref):
    S = k_ref.shape[0]
    _transpose_values(v_ref, vt_ref)
    for g in range(GQA_GROUP):
        qs_ref[g * S:(g + 1) * S, :] = q_ref[:, g * HEAD_DIM:(g + 1) * HEAD_DIM]

    def emit(i, o_even, o_odd):
        off = pl.multiple_of(i * 2 * ATT_RB, 2 * ATT_RB)
        os_ref[pl.ds(off, ATT_RB), :] = o_even.astype(os_ref.dtype)
        os_ref[pl.ds(off + ATT_RB, ATT_RB), :] = o_odd.astype(os_ref.dtype)

    _attention_rows(qs_ref, k_ref, vt_ref, (s0_ref, s1_ref), emit)
    for g in range(GQA_GROUP):
        o_ref[:, g * HEAD_DIM:(g + 1) * HEAD_DIM] = os_ref[g * S:(g + 1) * S, :]


def _gqa(proj, B, S):
    T = proj.shape[0]
    gw = GQA_GROUP * HEAD_DIM
    q0 = 0
    k0 = GQA_HEADS
    v0 = k0 + GQA_KV
    return pl.pallas_call(
        _gqa_kernel,
        grid=(B, GQA_KV),
        in_specs=[
            pl.BlockSpec((S, gw), lambda b, kh: (b, q0 + kh)),
            pl.BlockSpec((S, HEAD_DIM), lambda b, kh: (b, k0 + kh)),
            pl.BlockSpec((S, HEAD_DIM), lambda b, kh: (b, v0 + kh)),
        ],
        out_specs=pl.BlockSpec((S, gw), lambda b, kh: (b, kh)),
        out_shape=jax.ShapeDtypeStruct((T, GQA_HEADS * HEAD_DIM), BF16),
        scratch_shapes=[pltpu.VMEM((HEAD_DIM, S), BF16),
                        pltpu.VMEM((S, ATT_RB), F32), pltpu.VMEM((S, ATT_RB), F32),
                        pltpu.VMEM((GQA_GROUP * S, HEAD_DIM), BF16),
                        pltpu.VMEM((GQA_GROUP * S, HEAD_DIM), BF16)],
        compiler_params=_params("parallel", "parallel"),
        name="gqa",
    )(proj, proj, proj)


def _diff_kernel(lam_ref, gain_ref, q_ref, k_ref, v_ref, o_ref, vt_ref, s0_ref, s1_ref, qc_ref, *, lam_init):
    _transpose_values(v_ref, vt_ref)

    S = q_ref.shape[0]
    lane = lax.broadcasted_iota(jnp.int32, (ATT_RB, HEAD_DIM), 1)
    first = lane < DIFF_QK_DIM
    for r in range(S // ATT_RB):
        q = q_ref[r * ATT_RB:(r + 1) * ATT_RB, :]
        zero = jnp.zeros_like(q)
        qc_ref[2 * r * ATT_RB:(2 * r + 1) * ATT_RB, :] = jnp.where(first, q, zero)
        qc_ref[(2 * r + 1) * ATT_RB:(2 * r + 2) * ATT_RB, :] = jnp.where(first, zero, q)

    lp = lam_ref[...]
    lam = (jnp.exp(jnp.sum(lp[0:1] * lp[1:2], axis=-1, keepdims=True))
           - jnp.exp(jnp.sum(lp[2:3] * lp[3:4], axis=-1, keepdims=True)) + lam_init)

    def emit(i, o1, o2):
        off = pl.multiple_of(i * ATT_RB, ATT_RB)
        o = o1 - lam * o2
        o = _rms_rows(o, gain_ref[...]) * (1.0 - lam_init)
        o_ref[pl.ds(off, ATT_RB), :] = o.astype(o_ref.dtype)

    _attention_rows(qc_ref, k_ref, vt_ref, (s0_ref, s1_ref), emit)


def _diff(proj_qk, proj_plain, lam_params, gain, layer, B, S):
    T = proj_qk.shape[0]
    q0 = 0
    k0 = DIFF_HEADS
    v0 = (PLAIN_TILES - 1) * PROJ_TN // HEAD_DIM
    lam_init = 0.8 - 0.6 * math.exp(-0.3 * layer)
    return pl.pallas_call(
        functools.partial(_diff_kernel, lam_init=lam_init),
        grid=(B, DIFF_HEADS),
        in_specs=[
            pl.BlockSpec((4, DIFF_QK_DIM), lambda b, h: (0, 0)),
            pl.BlockSpec((1, HEAD_DIM), lambda b, h: (0, 0)),
            pl.BlockSpec((S, HEAD_DIM), lambda b, h: (b, q0 + h)),
            pl.BlockSpec((S, HEAD_DIM), lambda b, h: (b, k0 + h)),
            pl.BlockSpec((S, HEAD_DIM), lambda b, h: (b, v0 + h)),
        ],
        out_specs=pl.BlockSpec((S, HEAD_DIM), lambda b, h: (b, h)),
        out_shape=jax.ShapeDtypeStruct((T, DIFF_HEADS * HEAD_DIM), BF16),
        scratch_shapes=[pltpu.VMEM((HEAD_DIM, S), BF16),
                        pltpu.VMEM((S, ATT_RB), F32), pltpu.VMEM((S, ATT_RB), F32),
                        pltpu.VMEM((2 * S, HEAD_DIM), BF16)],
        compiler_params=_params("parallel", "parallel"),
        name="diff",
    )(lam_params, gain.reshape(1, HEAD_DIM), proj_qk, proj_qk, proj_plain)


def _na_window(blk, rows):
    r0 = blk * NA_QROWS
    return r0, int(np.clip(r0 - NA_KH // 2, 0, rows - NA_KROWS))


def _na_slab_index(r0, ws, k, j, rows):
    n_dr = 2 * NA_KH - 1
    kr = ws + k
    a, valid = [], []
    for r in (r0 + 2 * j, r0 + 2 * j + 1):
        rs = int(np.clip(r - NA_KH // 2, 0, rows - NA_KH))
        valid.append(rs <= kr < rs + NA_KH)
        a.append(kr - r + (NA_KH - 1))
    if valid[0] and valid[1]:
        return a[0] - 1
    if valid[0]:
        return (n_dr - 1) + a[0]
    if valid[1]:
        return (n_dr - 1) + n_dr + a[1]
    return None


def _na_kernel(q_ref, k_ref, v_ref, slab_ref, o_ref, vt_ref, *, rows):
    _transpose_values(v_ref, vt_ref)
    tq = NA_QROWS * GRID_W
    tk = NA_KROWS * GRID_W
    masked = jnp.full((GRID_W, 2 * GRID_W), NEG_BIG, F32)
    for blk in range(rows // NA_QROWS):
        r0, ws = _na_window(blk, rows)
        bias_rows = []
        for k in range(NA_KROWS):
            slabs = []
            for j in range(NA_QROWS // 2):
                idx = _na_slab_index(r0, ws, k, j, rows)
                slabs.append(masked if idx is None else slab_ref[idx])
            bias_rows.append(jnp.concatenate(slabs, axis=1))
        bias = jnp.concatenate(bias_rows, axis=0)
        q = q_ref[r0 * GRID_W:r0 * GRID_W + tq, :]
        kw = k_ref[ws * GRID_W:ws * GRID_W + tk, :]
        st = lax.dot_general(kw, q, (((1,), (1,)), ((), ())), preferred_element_type=F32) + bias
        m = jnp.max(st, axis=0, keepdims=True)
        p = jnp.exp2(st - m)
        l = jnp.sum(p, axis=0, keepdims=True)
        ot = jnp.dot(vt_ref[:, ws * GRID_W:ws * GRID_W + tk], p.astype(BF16), preferred_element_type=F32)
        o_ref[r0 * GRID_W:r0 * GRID_W + tq, :] = (ot / l).T.astype(o_ref.dtype)


def _na(proj, slabs, B, S, layer):
    T = proj.shape[0]
    rows = S // GRID_W
    n_slab = slabs.shape[2]
    return pl.pallas_call(
        functools.partial(_na_kernel, rows=rows),
        grid=(B, NA_HEADS),
        in_specs=[
            pl.BlockSpec((S, HEAD_DIM), lambda b, h: (b, h)),
            pl.BlockSpec((S, HEAD_DIM), lambda b, h: (b, NA_HEADS + h)),
            pl.BlockSpec((S, HEAD_DIM), lambda b, h: (b, 2 * NA_HEADS + h)),
            pl.BlockSpec((None, None, n_slab, GRID_W, 2 * GRID_W), lambda b, h: (layer, h, 0, 0, 0)),
        ],
        out_specs=pl.BlockSpec((S, HEAD_DIM), lambda b, h: (b, h)),
        out_shape=jax.ShapeDtypeStruct((T, NA_HEADS * HEAD_DIM), BF16),
        scratch_shapes=[pltpu.VMEM((HEAD_DIM, S), BF16)],
        compiler_params=_params("parallel", "parallel"),
        name="na",
    )(proj, proj, proj, slabs)


def _na_bias_slabs(rpb):
    n_dc = 2 * NA_KW - 1
    cq = np.arange(GRID_W)
    cs = np.clip(cq - NA_KW // 2, 0, GRID_W - NA_KW)
    ck = np.arange(GRID_W)
    valid_c = (ck[:, None] >= cs[None, :]) & (ck[:, None] < cs[None, :] + NA_KW)
    dc = ck[:, None] - cq[None, :] + (NA_KW - 1)
    onehot = ((dc[None] == np.arange(n_dc)[:, None, None]) & valid_c[None]).astype(np.float32)
    t = jnp.einsum('lhad,dkq->lhakq', rpb, jnp.asarray(onehot), precision=lax.Precision.HIGHEST)
    t = (t + jnp.asarray(np.where(valid_c, 0.0, NEG_BIG).astype(np.float32))) * LOG2E
    masked = jnp.full_like(t, NEG_BIG)
    both = jnp.concatenate([t[:, :, 1:], t[:, :, :-1]], axis=-1)
    left = jnp.concatenate([t, masked], axis=-1)
    right = jnp.concatenate([masked, t], axis=-1)
    return jnp.concatenate([both, left, right], axis=2)


def _outproj_kernel(h_ref, a_ref, b_ref, c_ref, w_ref, o_ref, w_bf_ref):
    @pl.when(pl.program_id(0) == 0)
    def _():
        w_bf_ref[...] = w_ref[...].astype(BF16)

    na = a_ref.shape[1]
    nb = b_ref.shape[1]
    acc = jnp.dot(a_ref[...], w_bf_ref[0:na, :], preferred_element_type=F32)
    acc += jnp.dot(b_ref[...], w_bf_ref[na:na + nb, :], preferred_element_type=F32)
    acc += jnp.dot(c_ref[...], w_bf_ref[na + nb:, :], preferred_element_type=F32)
    o_ref[...] = h_ref[...] + acc


def _outproj(h, oa, ob, oc, w, layer):
    T = h.shape[0]
    return pl.pallas_call(
        _outproj_kernel,
        grid=(T // OUT_TM,),
        in_specs=[
            pl.BlockSpec((OUT_TM, D_MODEL), lambda i: (i, 0)),
            pl.BlockSpec((OUT_TM, oa.shape[1]), lambda i: (i, 0)),
            pl.BlockSpec((OUT_TM, ob.shape[1]), lambda i: (i, 0)),
            pl.BlockSpec((OUT_TM, oc.shape[1]), lambda i: (i, 0)),
            pl.BlockSpec((None, D_MODEL, D_MODEL), lambda i: (layer, 0, 0), pipeline_mode=pl.Buffered(1)),
        ],
        out_specs=pl.BlockSpec((OUT_TM, D_MODEL), lambda i: (i, 0)),
        out_shape=jax.ShapeDtypeStruct((T, D_MODEL), F32),
        scratch_shapes=[pltpu.VMEM((D_MODEL, D_MODEL), BF16)],
        compiler_params=_params("arbitrary"),
        name="outproj",
    )(h, oa, ob, oc, w)


def _ple_kernel(h_ref, g_ref, wg_ref, p_ref, wp_ref, gf_ref, o_ref, wg_bf_ref, wp_bf_ref, *, final):
    @pl.when(pl.program_id(0) == 0)
    def _():
        wg_bf_ref[...] = wg_ref[...].astype(BF16)
        wp_bf_ref[...] = wp_ref[...].astype(BF16)

    x = h_ref[...]
    xn = _rms_rows(x, g_ref[...]).astype(BF16)
    gate = jax.nn.sigmoid(jnp.dot(xn, wg_bf_ref[...], preferred_element_type=F32))
    emb = jnp.dot(p_ref[...].astype(BF16), wp_bf_ref[...], preferred_element_type=F32)
    y = x + gate * emb
    if final:
        y = _rms_rows(y, gf_ref[...])
    o_ref[...] = y


def _ple(h, g, wg, p, wp, g_final, layer, final):
    T = h.shape[0]
    resident = pl.Buffered(1)
    return pl.pallas_call(
        functools.partial(_ple_kernel, final=final),
        grid=(T // PLE_TM,),
        in_specs=[
            pl.BlockSpec((PLE_TM, D_MODEL), lambda i: (i, 0)),
            pl.BlockSpec((1, D_MODEL), lambda i: (0, 0)),
            pl.BlockSpec((None, D_MODEL, D_MODEL), lambda i: (layer, 0, 0), pipeline_mode=resident),
            pl.BlockSpec((None, PLE_TM, PLE_DIM), lambda i: (layer, i, 0)),
            pl.BlockSpec((None, PLE_DIM, D_MODEL), lambda i: (layer, 0, 0), pipeline_mode=resident),
            pl.BlockSpec((1, D_MODEL), lambda i: (0, 0)),
        ],
        out_specs=pl.BlockSpec((PLE_TM, D_MODEL), lambda i: (i, 0)),
        out_shape=jax.ShapeDtypeStruct((T, D_MODEL), F32),
        scratch_shapes=[pltpu.VMEM((D_MODEL, D_MODEL), BF16), pltpu.VMEM((PLE_DIM, D_MODEL), BF16)],
        compiler_params=_params("arbitrary"),
        name="ple",
    )(h, g.reshape(1, D_MODEL), wg, p, wp, g_final.reshape(1, D_MODEL))


def _rope_tables(S):
    t = jnp.arange(S)
    inv_ax = jnp.power(ROPE_THETA, -jnp.arange(0, AXIAL_DIM, 2, dtype=F32) / AXIAL_DIM)
    ang_b = jnp.concatenate([(t // GRID_W).astype(F32)[:, None] * inv_ax,
                             (t % GRID_W).astype(F32)[:, None] * inv_ax], axis=-1)
    cos_b, sin_b = jnp.cos(ang_b), jnp.sin(ang_b)
    cb = jnp.concatenate([cos_b, cos_b], axis=-1)
    sb = jnp.concatenate([-sin_b, sin_b], axis=-1)
    inv_1d = jnp.power(ROPE_THETA, -jnp.arange(0, DIFF_QK_DIM, 2, dtype=F32) / DIFF_QK_DIM)
    ang_c = t.astype(F32)[:, None] * inv_1d
    cos_c, sin_c = jnp.cos(ang_c), jnp.sin(ang_c)
    zero = jnp.zeros_like(sin_c)
    cc = jnp.concatenate([cos_c] * 4, axis=-1)
    sca = jnp.concatenate([-sin_c, zero, -sin_c, zero], axis=-1)
    scb = jnp.concatenate([zero, sin_c, zero, sin_c], axis=-1)
    return cb, sb, cc, sca, scb


def kernel(x, p, g_ffn1, w1_gate, w1_up, w1_down, g_mix, w_in, na_rpb, gqa_q_gain, gqa_k_gain,
           diff_lambda, diff_subln_gain, w_out, g_ffn2, w2_gate, w2_up, w2_down, g_ple,
           w_ple_gate, w_ple_proj, g_final):
    B, S, D = x.shape
    depth = w_in.shape[0]
    T = B * S
    cb, sb, cc, sca, scb = _rope_tables(S)
    slabs = _na_bias_slabs(na_rpb)
    p_flat = p.reshape(depth, T, PLE_DIM)
    h = x.reshape(T, D)
    for i in range(depth):
        h = _ffn(h, g_ffn1[i], w1_gate, w1_up, w1_down, i)
        proj_plain, xn = _proj_plain(h, g_mix[i], w_in, i)
        q_gain = gqa_q_gain[i] * (HEAD_DIM ** -0.5 * LOG2E)
        proj_axial = _proj_axial(xn, w_in, q_gain, gqa_k_gain[i], cb, sb, S, i)
        proj_pair = _proj_pair(xn, w_in, cc, sca, scb, S, i)
        out_a = _na(proj_plain, slabs, B, S, i)
        out_b = _gqa(proj_axial, B, S)
        out_c = _diff(proj_pair, proj_plain, diff_lambda[i], diff_subln_gain[i], i, B, S)
        h = _outproj(h, out_a, out_b, out_c, w_out, i)
        h = _ffn(h, g_ffn2[i], w2_gate, w2_up, w2_down, i)
        h = _ple(h, g_ple[i], w_ple_gate, p_flat, w_ple_proj, g_final, i, final=(i == depth - 1))
    return h.reshape(B, S, D)
```

```python
import functools
import math

import numpy as np
import jax
import jax.numpy as jnp
from jax import lax
from jax.experimental import pallas as pl
from jax.experimental.pallas import tpu as pltpu

F32 = jnp.float32
BF16 = jnp.bfloat16

D_MODEL = 2048
EPS = 1e-6
GRID_W = 64
HEAD_DIM = 128
NA_HEADS = 4
NA_KH = 8
NA_KW = 16
GQA_HEADS = 8
GQA_KV = 2
GQA_GROUP = GQA_HEADS // GQA_KV
AXIAL_DIM = HEAD_DIM // 2
DIFF_HEADS = 4
DIFF_QK_DIM = HEAD_DIM // 2
IN_COLS = 4608
D_FF = 5632
PLE_DIM = 256
ROPE_THETA = 10000.0

NEG_BIG = -1e30
LOG2E = math.log2(math.e)
VMEM_LIMIT_BYTES = 56 * 1024 * 1024

FFN_TM = 1024
FFN_TF = 512
PROJ_TM = 512
OUT_TM = 512
PLE_TM = 512
ATT_RB = 512
ATT_KC = 512
NA_QROWS = 8
NA_KROWS = 16


def _params(*sem):
    return pltpu.CompilerParams(dimension_semantics=sem, vmem_limit_bytes=VMEM_LIMIT_BYTES)


def _rms_rows(x, g):
    return x * lax.rsqrt(jnp.mean(x * x, axis=-1, keepdims=True) + EPS) * g


def _ffn_kernel(h_ref, g_ref, wg_ref, wu_ref, wd_ref, o_ref, xn_ref):
    @pl.when(pl.program_id(1) == 0)
    def _():
        x = h_ref[...]
        xn_ref[...] = _rms_rows(x, g_ref[...]).astype(BF16)
        o_ref[...] = x

    xn = xn_ref[...]
    a = jnp.dot(xn, wg_ref[...].astype(BF16), preferred_element_type=F32)
    u = jnp.dot(xn, wu_ref[...].astype(BF16), preferred_element_type=F32)
    act = (a * jax.nn.sigmoid(a) * (0.5 * u)).astype(BF16)
    o_ref[...] += jnp.dot(act, wd_ref[...].astype(BF16), preferred_element_type=F32)


def _ffn(h, g, wg, wu, wd, layer):
    T = h.shape[0]
    return pl.pallas_call(
        _ffn_kernel,
        grid=(T // FFN_TM, D_FF // FFN_TF),
        in_specs=[
            pl.BlockSpec((FFN_TM, D_MODEL), lambda i, j: (i, 0), pipeline_mode=pl.Buffered(1)),
            pl.BlockSpec((1, D_MODEL), lambda i, j: (0, 0)),
            pl.BlockSpec((None, D_MODEL, FFN_TF), lambda i, j: (layer, 0, j)),
            pl.BlockSpec((None, D_MODEL, FFN_TF), lambda i, j: (layer, 0, j)),
            pl.BlockSpec((None, FFN_TF, D_MODEL), lambda i, j: (layer, j, 0)),
        ],
        out_specs=pl.BlockSpec((FFN_TM, D_MODEL), lambda i, j: (i, 0)),
        out_shape=jax.ShapeDtypeStruct((T, D_MODEL), F32),
        scratch_shapes=[pltpu.VMEM((FFN_TM, D_MODEL), BF16)],
        compiler_params=pltpu.CompilerParams(dimension_semantics=("parallel", "arbitrary"),
                                             vmem_limit_bytes=60 * 1024 * 1024),
        name="ffn",
    )(h, g.reshape(1, D_MODEL), wg, wu, wd)


PROJ_GROUP = IN_COLS // 3
EPILOGUE_ROWS = 256


def _round_weight_once(w_ref, w_bf_ref):
    @pl.when(pl.program_id(0) == 0)
    def _():
        w_bf_ref[...] = w_ref[...].astype(BF16)


def _proj_na_kernel(h_ref, g_ref, w_ref, o_ref, xn_ref, w_bf_ref):
    _round_weight_once(w_ref, w_bf_ref)
    xn = _rms_rows(h_ref[...], g_ref[...]).astype(BF16)
    xn_ref[...] = xn
    y = jnp.dot(xn, w_bf_ref[...], preferred_element_type=F32)
    nq = NA_HEADS * HEAD_DIM
    o_ref[:, :nq] = (y[:, :nq] * (HEAD_DIM ** -0.5 * LOG2E)).astype(BF16)
    o_ref[:, nq:] = y[:, nq:].astype(BF16)


def _proj_gqa_kernel(xn_ref, w_ref, qg_ref, kg_ref, cb_ref, sb_ref, o_ref, w_bf_ref, y_ref):
    _round_weight_once(w_ref, w_bf_ref)
    y_ref[...] = jnp.dot(xn_ref[...], w_bf_ref[...], preferred_element_type=F32)
    n_roped = GQA_HEADS + GQA_KV

    def chunk(r, carry):
        rows = pl.ds(pl.multiple_of(r * EPILOGUE_ROWS, EPILOGUE_ROWS), EPILOGUE_ROWS)
        cb = cb_ref[rows, :]
        sb = sb_ref[rows, :]
        for k in range(n_roped):
            gain = qg_ref[...] if k < GQA_HEADS else kg_ref[...]
            xg = _rms_rows(y_ref[rows, k * HEAD_DIM:(k + 1) * HEAD_DIM], gain)
            r_out = xg * cb + pltpu.roll(xg, HEAD_DIM // 2, 1) * sb
            o_ref[rows, k * HEAD_DIM:(k + 1) * HEAD_DIM] = r_out.astype(BF16)
        o_ref[rows, n_roped * HEAD_DIM:] = y_ref[rows, n_roped * HEAD_DIM:].astype(BF16)
        return carry

    lax.fori_loop(0, PROJ_TM // EPILOGUE_ROWS, chunk, 0)


def _proj_diff_kernel(xn_ref, w_ref, cc_ref, sca_ref, scb_ref, o_ref, w_bf_ref):
    _round_weight_once(w_ref, w_bf_ref)
    y = jnp.dot(xn_ref[...], w_bf_ref[...], preferred_element_type=F32)
    for k in range(2 * DIFF_HEADS):
        x = y[:, k * HEAD_DIM:(k + 1) * HEAD_DIM]
        r = (x * cc_ref[...] + pltpu.roll(x, HEAD_DIM - DIFF_QK_DIM // 2, 1) * sca_ref[...]
             + pltpu.roll(x, DIFF_QK_DIM // 2, 1) * scb_ref[...])
        if k < DIFF_HEADS:
            r = r * (DIFF_QK_DIM ** -0.5 * LOG2E)
        o_ref[:, k * HEAD_DIM:(k + 1) * HEAD_DIM] = r.astype(BF16)
    o_ref[:, 2 * DIFF_HEADS * HEAD_DIM:] = y[:, 2 * DIFF_HEADS * HEAD_DIM:].astype(BF16)


def _token_tile(width):
    return pl.BlockSpec((PROJ_TM, width), lambda i: (i, 0))


def _weight_group(layer, group):
    return pl.BlockSpec((None, D_MODEL, PROJ_GROUP), lambda i: (layer, 0, group), pipeline_mode=pl.Buffered(1))


def _proj_na(h, g, w, layer):
    T = h.shape[0]
    return pl.pallas_call(
        _proj_na_kernel,
        grid=(T // PROJ_TM,),
        in_specs=[_token_tile(D_MODEL), pl.BlockSpec((1, D_MODEL), lambda i: (0, 0)), _weight_group(layer, 0)],
        out_specs=[_token_tile(PROJ_GROUP), _token_tile(D_MODEL)],
        out_shape=[jax.ShapeDtypeStruct((T, PROJ_GROUP), BF16), jax.ShapeDtypeStruct((T, D_MODEL), BF16)],
        scratch_shapes=[pltpu.VMEM((D_MODEL, PROJ_GROUP), BF16)],
        compiler_params=_params("arbitrary"),
        name="proj_na",
    )(h, g.reshape(1, D_MODEL), w)


def _proj_gqa(xn, w, qg, kg, cb, sb, S, layer):
    T = xn.shape[0]
    nst = S // PROJ_TM
    table = pl.BlockSpec((PROJ_TM, HEAD_DIM), lambda i: (i % nst, 0))
    vec = pl.BlockSpec((1, HEAD_DIM), lambda i: (0, 0))
    return pl.pallas_call(
        _proj_gqa_kernel,
        grid=(T // PROJ_TM,),
        in_specs=[_token_tile(D_MODEL), _weight_group(layer, 1), vec, vec, table, table],
        out_specs=_token_tile(PROJ_GROUP),
        out_shape=jax.ShapeDtypeStruct((T, PROJ_GROUP), BF16),
        scratch_shapes=[pltpu.VMEM((D_MODEL, PROJ_GROUP), BF16), pltpu.VMEM((PROJ_TM, PROJ_GROUP), F32)],
        compiler_params=_params("arbitrary"),
        name="proj_gqa",
    )(xn, w, qg.reshape(1, HEAD_DIM), kg.reshape(1, HEAD_DIM), cb, sb)


def _proj_diff(xn, w, cc, sca, scb, S, layer):
    T = xn.shape[0]
    nst = S // PROJ_TM
    table = pl.BlockSpec((PROJ_TM, HEAD_DIM), lambda i: (i % nst, 0))
    return pl.pallas_call(
        _proj_diff_kernel,
        grid=(T // PROJ_TM,),
        in_specs=[_token_tile(D_MODEL), _weight_group(layer, 2), table, table, table],
        out_specs=_token_tile(PROJ_GROUP),
        out_shape=jax.ShapeDtypeStruct((T, PROJ_GROUP), BF16),
        scratch_shapes=[pltpu.VMEM((D_MODEL, PROJ_GROUP), BF16)],
        compiler_params=_params("arbitrary"),
        name="proj_diff",
    )(xn, w, cc, sca, scb)


def _transpose_values(v_ref, vt_ref):
    S = v_ref.shape[0]
    step = 512
    for c in range(S // step):
        blk = v_ref[c * step:(c + 1) * step, :].astype(F32)
        vt_ref[:, c * step:(c + 1) * step] = blk.T.astype(BF16)


def _attention_rows(q_ref, k_ref, vt_ref, s_refs, emit):
    npair = q_ref.shape[0] // (2 * ATT_RB)
    S = k_ref.shape[0]

    def scores(r, slot):
        off = pl.multiple_of(r * ATT_RB, ATT_RB)
        st = lax.dot_general(k_ref[...], q_ref[pl.ds(off, ATT_RB), :], (((1,), (1,)), ((), ())),
                             preferred_element_type=F32)
        s_refs[slot][...] = st
        return jnp.max(st, axis=0, keepdims=True)

    def consume(slot, m):
        acc = jnp.zeros((HEAD_DIM, ATT_RB), F32)
        l = jnp.zeros((1, ATT_RB), F32)
        for c in range(S // ATT_KC):
            p = jnp.exp2(s_refs[slot][c * ATT_KC:(c + 1) * ATT_KC, :] - m)
            l = l + jnp.sum(p, axis=0, keepdims=True)
            acc = acc + jnp.dot(vt_ref[:, c * ATT_KC:(c + 1) * ATT_KC], p.astype(BF16),
                                preferred_element_type=F32)
        return (acc / l).T

    def body(i, m_even):
        m_odd = scores(2 * i + 1, 1)
        o_even = consume(0, m_even)
        m_next = scores(2 * i + 2, 0)
        o_odd = consume(1, m_odd)
        emit(i, o_even, o_odd)
        return m_next

    m_even = lax.fori_loop(0, npair - 1, body, scores(0, 0))
    m_odd = scores(2 * npair - 1, 1)
    o_even = consume(0, m_even)
    o_odd = consume(1, m_odd)
    emit(npair - 1, o_even, o_odd)


def _gqa_kernel(q_ref, k_ref, v_ref, o_ref, vt_ref, s0_ref, s1_ref, qs_ref, os_ref):
    S = k_ref.shape[0]
    _transpose_values(v_ref, vt_ref)
    for g in range(GQA_GROUP):
        qs_ref[g * S:(g + 1) * S, :] = q_ref[:, g * HEAD_DIM:(g + 1) * HEAD_DIM]

    def emit(i, o_even, o_odd):
        off = pl.multiple_of(i * 2 * ATT_RB, 2 * ATT_RB)
        os_ref[pl.ds(off, ATT_RB), :] = o_even.astype(os_ref.dtype)
        os_ref[pl.ds(off + ATT_RB, ATT_RB), :] = o_odd.astype(os_ref.dtype)

    _attention_rows(qs_ref, k_ref, vt_ref, (s0_ref, s1_ref), emit)
    for g in range(GQA_GROUP):
        o_ref[:, g * HEAD_DIM:(g + 1) * HEAD_DIM] = os_ref[g * S:(g + 1) * S, :]


def _gqa(proj, B, S):
    T = proj.shape[0]
    gw = GQA_GROUP * HEAD_DIM
    q0 = 0
    k0 = GQA_HEADS
    v0 = k0 + GQA_KV
    return pl.pallas_call(
        _gqa_kernel,
        grid=(B, GQA_KV),
        in_specs=[
            pl.BlockSpec((S, gw), lambda b, kh: (b, q0 + kh)),
            pl.BlockSpec((S, HEAD_DIM), lambda b, kh: (b, k0 + kh)),
            pl.BlockSpec((S, HEAD_DIM), lambda b, kh: (b, v0 + kh)),
        ],
        out_specs=pl.BlockSpec((S, gw), lambda b, kh: (b, kh)),
        out_shape=jax.ShapeDtypeStruct((T, GQA_HEADS * HEAD_DIM), BF16),
        scratch_shapes=[pltpu.VMEM((HEAD_DIM, S), BF16),
                        pltpu.VMEM((S, ATT_RB), F32), pltpu.VMEM((S, ATT_RB), F32),
                        pltpu.VMEM((GQA_GROUP * S, HEAD_DIM), BF16),
                        pltpu.VMEM((GQA_GROUP * S, HEAD_DIM), BF16)],
        compiler_params=_params("parallel", "parallel"),
        name="gqa",
    )(proj, proj, proj)


def _diff_kernel(lam_ref, gain_ref, q_ref, k_ref, v_ref, o_ref, vt_ref, s0_ref, s1_ref, qc_ref, *, lam_init):
    _transpose_values(v_ref, vt_ref)

    S = q_ref.shape[0]
    lane = lax.broadcasted_iota(jnp.int32, (ATT_RB, HEAD_DIM), 1)
    first = lane < DIFF_QK_DIM
    for r in range(S // ATT_RB):
        q = q_ref[r * ATT_RB:(r + 1) * ATT_RB, :]
        zero = jnp.zeros_like(q)
        qc_ref[2 * r * ATT_RB:(2 * r + 1) * ATT_RB, :] = jnp.where(first, q, zero)
        qc_ref[(2 * r + 1) * ATT_RB:(2 * r + 2) * ATT_RB, :] = jnp.where(first, zero, q)

    lp = lam_ref[...]
    lam = (jnp.exp(jnp.sum(lp[0:1] * lp[1:2], axis=-1, keepdims=True))
           - jnp.exp(jnp.sum(lp[2:3] * lp[3:4], axis=-1, keepdims=True)) + lam_init)

    def emit(i, o1, o2):
        off = pl.multiple_of(i * ATT_RB, ATT_RB)
        o = o1 - lam * o2
        o = _rms_rows(o, gain_ref[...]) * (1.0 - lam_init)
        o_ref[pl.ds(off, ATT_RB), :] = o.astype(o_ref.dtype)

    _attention_rows(qc_ref, k_ref, vt_ref, (s0_ref, s1_ref), emit)


def _diff(proj, lam_params, gain, layer, B, S):
    T = proj.shape[0]
    q0 = 0
    k0 = DIFF_HEADS
    v0 = 2 * DIFF_HEADS
    lam_init = 0.8 - 0.6 * math.exp(-0.3 * layer)
    return pl.pallas_call(
        functools.partial(_diff_kernel, lam_init=lam_init),
        grid=(B, DIFF_HEADS),
        in_specs=[
            pl.BlockSpec((4, DIFF_QK_DIM), lambda b, h: (0, 0)),
            pl.BlockSpec((1, HEAD_DIM), lambda b, h: (0, 0)),
            pl.BlockSpec((S, HEAD_DIM), lambda b, h: (b, q0 + h)),
            pl.BlockSpec((S, HEAD_DIM), lambda b, h: (b, k0 + h)),
            pl.BlockSpec((S, HEAD_DIM), lambda b, h: (b, v0 + h)),
        ],
        out_specs=pl.BlockSpec((S, HEAD_DIM), lambda b, h: (b, h)),
        out_shape=jax.ShapeDtypeStruct((T, DIFF_HEADS * HEAD_DIM), BF16),
        scratch_shapes=[pltpu.VMEM((HEAD_DIM, S), BF16),
                        pltpu.VMEM((S, ATT_RB), F32), pltpu.VMEM((S, ATT_RB), F32),
                        pltpu.VMEM((2 * S, HEAD_DIM), BF16)],
        compiler_params=_params("parallel", "parallel"),
        name="diff",
    )(lam_params, gain.reshape(1, HEAD_DIM), proj, proj, proj)


def _na_window(blk, rows):
    r0 = blk * NA_QROWS
    return r0, int(np.clip(r0 - NA_KH // 2, 0, rows - NA_KROWS))


def _na_slab_index(r0, ws, k, j, rows):
    n_dr = 2 * NA_KH - 1
    kr = ws + k
    a, valid = [], []
    for r in (r0 + 2 * j, r0 + 2 * j + 1):
        rs = int(np.clip(r - NA_KH // 2, 0, rows - NA_KH))
        valid.append(rs <= kr < rs + NA_KH)
        a.append(kr - r + (NA_KH - 1))
    if valid[0] and valid[1]:
        return a[0] - 1
    if valid[0]:
        return (n_dr - 1) + a[0]
    if valid[1]:
        return (n_dr - 1) + n_dr + a[1]
    return None


def _na_kernel(q_ref, k_ref, v_ref, slab_ref, o_ref, vt_ref, *, rows):
    _transpose_values(v_ref, vt_ref)
    tq = NA_QROWS * GRID_W
    tk = NA_KROWS * GRID_W
    masked = jnp.full((GRID_W, 2 * GRID_W), NEG_BIG, F32)
    for blk in range(rows // NA_QROWS):
        r0, ws = _na_window(blk, rows)
        bias_rows = []
        for k in range(NA_KROWS):
            slabs = []
            for j in range(NA_QROWS // 2):
                idx = _na_slab_index(r0, ws, k, j, rows)
                slabs.append(masked if idx is None else slab_ref[idx])
            bias_rows.append(jnp.concatenate(slabs, axis=1))
        bias = jnp.concatenate(bias_rows, axis=0)
        q = q_ref[r0 * GRID_W:r0 * GRID_W + tq, :]
        kw = k_ref[ws * GRID_W:ws * GRID_W + tk, :]
        st = lax.dot_general(kw, q, (((1,), (1,)), ((), ())), preferred_element_type=F32) + bias
        m = jnp.max(st, axis=0, keepdims=True)
        p = jnp.exp2(st - m)
        l = jnp.sum(p, axis=0, keepdims=True)
        ot = jnp.dot(vt_ref[:, ws * GRID_W:ws * GRID_W + tk], p.astype(BF16), preferred_element_type=F32)
        o_ref[r0 * GRID_W:r0 * GRID_W + tq, :] = (ot / l).T.astype(o_ref.dtype)


def _na(proj, slabs, B, S, layer):
    T = proj.shape[0]
    rows = S // GRID_W
    n_slab = slabs.shape[2]
    return pl.pallas_call(
        functools.partial(_na_kernel, rows=rows),
        grid=(B, NA_HEADS),
        in_specs=[
            pl.BlockSpec((S, HEAD_DIM), lambda b, h: (b, h)),
            pl.BlockSpec((S, HEAD_DIM), lambda b, h: (b, NA_HEADS + h)),
            pl.BlockSpec((S, HEAD_DIM), lambda b, h: (b, 2 * NA_HEADS + h)),
            pl.BlockSpec((None, None, n_slab, GRID_W, 2 * GRID_W), lambda b, h: (layer, h, 0, 0, 0)),
        ],
        out_specs=pl.BlockSpec((S, HEAD_DIM), lambda b, h: (b, h)),
        out_shape=jax.ShapeDtypeStruct((T, NA_HEADS * HEAD_DIM), BF16),
        scratch_shapes=[pltpu.VMEM((HEAD_DIM, S), BF16)],
        compiler_params=_params("parallel", "parallel"),
        name="na",
    )(proj, proj, proj, slabs)


def _na_bias_slabs(rpb):
    n_dc = 2 * NA_KW - 1
    cq = np.arange(GRID_W)
    cs = np.clip(cq - NA_KW // 2, 0, GRID_W - NA_KW)
    ck = np.arange(GRID_W)
    valid_c = (ck[:, None] >= cs[None, :]) & (ck[:, None] < cs[None, :] + NA_KW)
    dc = ck[:, None] - cq[None, :] + (NA_KW - 1)
    onehot = ((dc[None] == np.arange(n_dc)[:, None, None]) & valid_c[None]).astype(np.float32)
    t = jnp.einsum('lhad,dkq->lhakq', rpb, jnp.asarray(onehot), precision=lax.Precision.HIGHEST)
    t = (t + jnp.asarray(np.where(valid_c, 0.0, NEG_BIG).astype(np.float32))) * LOG2E
    masked = jnp.full_like(t, NEG_BIG)
    both = jnp.concatenate([t[:, :, 1:], t[:, :, :-1]], axis=-1)
    left = jnp.concatenate([t, masked], axis=-1)
    right = jnp.concatenate([masked, t], axis=-1)
    return jnp.concatenate([both, left, right], axis=2)


def _outproj_kernel(h_ref, a_ref, b_ref, c_ref, w_ref, o_ref, w_bf_ref):
    @pl.when(pl.program_id(0) == 0)
    def _():
        w_bf_ref[...] = w_ref[...].astype(BF16)

    na = a_ref.shape[1]
    nb = b_ref.shape[1]
    acc = jnp.dot(a_ref[...], w_bf_ref[0:na, :], preferred_element_type=F32)
    acc += jnp.dot(b_ref[...], w_bf_ref[na:na + nb, :], preferred_element_type=F32)
    acc += jnp.dot(c_ref[...], w_bf_ref[na + nb:, :], preferred_element_type=F32)
    o_ref[...] = h_ref[...] + acc


def _outproj(h, oa, ob, oc, w, layer):
    T = h.shape[0]
    return pl.pallas_call(
        _outproj_kernel,
        grid=(T // OUT_TM,),
        in_specs=[
            pl.BlockSpec((OUT_TM, D_MODEL), lambda i: (i, 0)),
            pl.BlockSpec((OUT_TM, oa.shape[1]), lambda i: (i, 0)),
            pl.BlockSpec((OUT_TM, ob.shape[1]), lambda i: (i, 0)),
            pl.BlockSpec((OUT_TM, oc.shape[1]), lambda i: (i, 0)),
            pl.BlockSpec((None, D_MODEL, D_MODEL), lambda i: (layer, 0, 0), pipeline_mode=pl.Buffered(1)),
        ],
        out_specs=pl.BlockSpec((OUT_TM, D_MODEL), lambda i: (i, 0)),
        out_shape=jax.ShapeDtypeStruct((T, D_MODEL), F32),
        scratch_shapes=[pltpu.VMEM((D_MODEL, D_MODEL), BF16)],
        compiler_params=_params("arbitrary"),
        name="outproj",
    )(h, oa, ob, oc, w)


def _ple_kernel(h_ref, g_ref, wg_ref, p_ref, wp_ref, gf_ref, o_ref, wg_bf_ref, wp_bf_ref, *, final):
    @pl.when(pl.program_id(0) == 0)
    def _():
        wg_bf_ref[...] = wg_ref[...].astype(BF16)
        wp_bf_ref[...] = wp_ref[...].astype(BF16)

    x = h_ref[...]
    xn = _rms_rows(x, g_ref[...]).astype(BF16)
    gate = jax.nn.sigmoid(jnp.dot(xn, wg_bf_ref[...], preferred_element_type=F32))
    emb = jnp.dot(p_ref[...].astype(BF16), wp_bf_ref[...], preferred_element_type=F32)
    y = x + gate * emb
    if final:
        y = _rms_rows(y, gf_ref[...])
    o_ref[...] = y


def _ple(h, g, wg, p, wp, g_final, layer, final):
    T = h.shape[0]
    resident = pl.Buffered(1)
    return pl.pallas_call(
        functools.partial(_ple_kernel, final=final),
        grid=(T // PLE_TM,),
        in_specs=[
            pl.BlockSpec((PLE_TM, D_MODEL), lambda i: (i, 0)),
            pl.BlockSpec((1, D_MODEL), lambda i: (0, 0)),
            pl.BlockSpec((None, D_MODEL, D_MODEL), lambda i: (layer, 0, 0), pipeline_mode=resident),
            pl.BlockSpec((None, PLE_TM, PLE_DIM), lambda i: (layer, i, 0)),
            pl.BlockSpec((None, PLE_DIM, D_MODEL), lambda i: (layer, 0, 0), pipeline_mode=resident),
            pl.BlockSpec((1, D_MODEL), lambda i: (0, 0)),
        ],
        out_specs=pl.BlockSpec((PLE_TM, D_MODEL), lambda i: (i, 0)),
        out_shape=jax.ShapeDtypeStruct((T, D_MODEL), F32),
        scratch_shapes=[pltpu.VMEM((D_MODEL, D_MODEL), BF16), pltpu.VMEM((PLE_DIM, D_MODEL), BF16)],
        compiler_params=_params("arbitrary"),
        name="ple",
    )(h, g.reshape(1, D_MODEL), wg, p, wp, g_final.reshape(1, D_MODEL))


def _rope_tables(S):
    t = jnp.arange(S)
    inv_ax = jnp.power(ROPE_THETA, -jnp.arange(0, AXIAL_DIM, 2, dtype=F32) / AXIAL_DIM)
    ang_b = jnp.concatenate([(t // GRID_W).astype(F32)[:, None] * inv_ax,
                             (t % GRID_W).astype(F32)[:, None] * inv_ax], axis=-1)
    cos_b, sin_b = jnp.cos(ang_b), jnp.sin(ang_b)
    cb = jnp.concatenate([cos_b, cos_b], axis=-1)
    sb = jnp.concatenate([-sin_b, sin_b], axis=-1)
    inv_1d = jnp.power(ROPE_THETA, -jnp.arange(0, DIFF_QK_DIM, 2, dtype=F32) / DIFF_QK_DIM)
    ang_c = t.astype(F32)[:, None] * inv_1d
    cos_c, sin_c = jnp.cos(ang_c), jnp.sin(ang_c)
    zero = jnp.zeros_like(sin_c)
    cc = jnp.concatenate([cos_c] * 4, axis=-1)
    sca = jnp.concatenate([-sin_c, zero, -sin_c, zero], axis=-1)
    scb = jnp.concatenate([zero, sin_c, zero, sin_c], axis=-1)
    return cb, sb, cc, sca, scb


def kernel(x, p, g_ffn1, w1_gate, w1_up, w1_down, g_mix, w_in, na_rpb, gqa_q_gain, gqa_k_gain,
           diff_lambda, diff_subln_gain, w_out, g_ffn2, w2_gate, w2_up, w2_down, g_ple,
           w_ple_gate, w_ple_proj, g_final):
    B, S, D = x.shape
    depth = w_in.shape[0]
    T = B * S
    cb, sb, cc, sca, scb = _rope_tables(S)
    slabs = _na_bias_slabs(na_rpb)
    p_flat = p.reshape(depth, T, PLE_DIM)
    h = x.reshape(T, D)
    for i in range(depth):
        h = _ffn(h, g_ffn1[i], w1_gate, w1_up, w1_down, i)
        proj_a, xn = _proj_na(h, g_mix[i], w_in, i)
        q_gain = gqa_q_gain[i] * (HEAD_DIM ** -0.5 * LOG2E)
        proj_b = _proj_gqa(xn, w_in, q_gain, gqa_k_gain[i], cb, sb, S, i)
        proj_c = _proj_diff(xn, w_in, cc, sca, scb, S, i)
        out_a = _na(proj_a, slabs, B, S, i)
        out_b = _gqa(proj_b, B, S)
        out_c = _diff(proj_c, diff_lambda[i], diff_subln_gain[i], i, B, S)
        h = _outproj(h, out_a, out_b, out_c, w_out, i)
        h = _ffn(h, g_ffn2[i], w2_gate, w2_up, w2_down, i)
        h = _ple(h, g_ple[i], w_ple_gate, p_flat, w_ple_proj, g_final, i, final=(i == depth - 1))
    return h.reshape(B, S, D)
```

```python
import functools
import math

import numpy as np
import jax
import jax.numpy as jnp
from jax import lax
from jax.experimental import pallas as pl
from jax.experimental.pallas import tpu as pltpu

F32 = jnp.float32
BF16 = jnp.bfloat16

D_MODEL = 2048
EPS = 1e-6
GRID_W = 64
HEAD_DIM = 128
NA_HEADS = 4
NA_KH = 8
NA_KW = 16
GQA_HEADS = 8
GQA_KV = 2
GQA_GROUP = GQA_HEADS // GQA_KV
AXIAL_DIM = HEAD_DIM // 2
DIFF_HEADS = 4
DIFF_QK_DIM = HEAD_DIM // 2
IN_COLS = 4608
D_FF = 5632
PLE_DIM = 256
ROPE_THETA = 10000.0

NEG_BIG = -1e30
LOG2E = math.log2(math.e)
VMEM_LIMIT_BYTES = 56 * 1024 * 1024

FFN_TM = 1024
FFN_TF = 512
PROJ_TM = 512
OUT_TM = 512
PLE_TM = 512
ATT_RB = 512
ATT_KC = 512
NA_QROWS = 8
NA_KROWS = 16


def _params(*sem):
    return pltpu.CompilerParams(dimension_semantics=sem, vmem_limit_bytes=VMEM_LIMIT_BYTES)


def _rms_rows(x, g):
    return x * lax.rsqrt(jnp.mean(x * x, axis=-1, keepdims=True) + EPS) * g


def _ffn_kernel(h_ref, g_ref, wg_ref, wu_ref, wd_ref, o_ref, xn_ref):
    @pl.when(pl.program_id(1) == 0)
    def _():
        x = h_ref[...]
        xn_ref[...] = _rms_rows(x, g_ref[...]).astype(BF16)
        o_ref[...] = x

    xn = xn_ref[...]
    a = jnp.dot(xn, wg_ref[...].astype(BF16), preferred_element_type=F32)
    u = jnp.dot(xn, wu_ref[...].astype(BF16), preferred_element_type=F32)
    act = (a * jax.nn.sigmoid(a) * (0.5 * u)).astype(BF16)
    o_ref[...] += jnp.dot(act, wd_ref[...].astype(BF16), preferred_element_type=F32)


def _ffn(h, g, wg, wu, wd, layer):
    T = h.shape[0]
    return pl.pallas_call(
        _ffn_kernel,
        grid=(T // FFN_TM, D_FF // FFN_TF),
        in_specs=[
            pl.BlockSpec((FFN_TM, D_MODEL), lambda i, j: (i, 0), pipeline_mode=pl.Buffered(1)),
            pl.BlockSpec((1, D_MODEL), lambda i, j: (0, 0)),
            pl.BlockSpec((None, D_MODEL, FFN_TF), lambda i, j: (layer, 0, j)),
            pl.BlockSpec((None, D_MODEL, FFN_TF), lambda i, j: (layer, 0, j)),
            pl.BlockSpec((None, FFN_TF, D_MODEL), lambda i, j: (layer, j, 0)),
        ],
        out_specs=pl.BlockSpec((FFN_TM, D_MODEL), lambda i, j: (i, 0)),
        out_shape=jax.ShapeDtypeStruct((T, D_MODEL), F32),
        scratch_shapes=[pltpu.VMEM((FFN_TM, D_MODEL), BF16)],
        compiler_params=pltpu.CompilerParams(dimension_semantics=("parallel", "arbitrary"),
                                             vmem_limit_bytes=60 * 1024 * 1024),
        name="ffn",
    )(h, g.reshape(1, D_MODEL), wg, wu, wd)


PROJ_GROUP = IN_COLS // 3


def _round_weight_once(w_ref, w_bf_ref):
    @pl.when(pl.program_id(0) == 0)
    def _():
        w_bf_ref[...] = w_ref[...].astype(BF16)


def _proj_na_kernel(h_ref, g_ref, w_ref, o_ref, xn_ref, w_bf_ref):
    _round_weight_once(w_ref, w_bf_ref)
    xn = _rms_rows(h_ref[...], g_ref[...]).astype(BF16)
    xn_ref[...] = xn
    y = jnp.dot(xn, w_bf_ref[...], preferred_element_type=F32)
    nq = NA_HEADS * HEAD_DIM
    o_ref[:, :nq] = (y[:, :nq] * (HEAD_DIM ** -0.5 * LOG2E)).astype(BF16)
    o_ref[:, nq:] = y[:, nq:].astype(BF16)


def _proj_gqa_kernel(xn_ref, w_ref, qg_ref, kg_ref, cb_ref, sb_ref, o_ref, w_bf_ref, y_ref):
    _round_weight_once(w_ref, w_bf_ref)
    y_ref[...] = jnp.dot(xn_ref[...], w_bf_ref[...], preferred_element_type=F32)
    n_roped = GQA_HEADS + GQA_KV

    @pl.when(pl.program_id(0) < pl.num_programs(0))
    def _():
        for k in range(n_roped):
            gain = qg_ref[...] if k < GQA_HEADS else kg_ref[...]
            xg = _rms_rows(y_ref[:, k * HEAD_DIM:(k + 1) * HEAD_DIM], gain)
            r_out = xg * cb_ref[...] + pltpu.roll(xg, HEAD_DIM // 2, 1) * sb_ref[...]
            o_ref[:, k * HEAD_DIM:(k + 1) * HEAD_DIM] = r_out.astype(BF16)
        o_ref[:, n_roped * HEAD_DIM:] = y_ref[:, n_roped * HEAD_DIM:].astype(BF16)


def _proj_diff_kernel(xn_ref, w_ref, cc_ref, sca_ref, scb_ref, o_ref, w_bf_ref):
    _round_weight_once(w_ref, w_bf_ref)
    y = jnp.dot(xn_ref[...], w_bf_ref[...], preferred_element_type=F32)
    for k in range(2 * DIFF_HEADS):
        x = y[:, k * HEAD_DIM:(k + 1) * HEAD_DIM]
        r = (x * cc_ref[...] + pltpu.roll(x, HEAD_DIM - DIFF_QK_DIM // 2, 1) * sca_ref[...]
             + pltpu.roll(x, DIFF_QK_DIM // 2, 1) * scb_ref[...])
        if k < DIFF_HEADS:
            r = r * (DIFF_QK_DIM ** -0.5 * LOG2E)
        o_ref[:, k * HEAD_DIM:(k + 1) * HEAD_DIM] = r.astype(BF16)
    o_ref[:, 2 * DIFF_HEADS * HEAD_DIM:] = y[:, 2 * DIFF_HEADS * HEAD_DIM:].astype(BF16)


def _token_tile(width):
    return pl.BlockSpec((PROJ_TM, width), lambda i: (i, 0))


def _weight_group(layer, group):
    return pl.BlockSpec((None, D_MODEL, PROJ_GROUP), lambda i: (layer, 0, group), pipeline_mode=pl.Buffered(1))


def _proj_na(h, g, w, layer):
    T = h.shape[0]
    return pl.pallas_call(
        _proj_na_kernel,
        grid=(T // PROJ_TM,),
        in_specs=[_token_tile(D_MODEL), pl.BlockSpec((1, D_MODEL), lambda i: (0, 0)), _weight_group(layer, 0)],
        out_specs=[_token_tile(PROJ_GROUP), _token_tile(D_MODEL)],
        out_shape=[jax.ShapeDtypeStruct((T, PROJ_GROUP), BF16), jax.ShapeDtypeStruct((T, D_MODEL), BF16)],
        scratch_shapes=[pltpu.VMEM((D_MODEL, PROJ_GROUP), BF16)],
        compiler_params=_params("arbitrary"),
        name="proj_na",
    )(h, g.reshape(1, D_MODEL), w)


def _proj_gqa(xn, w, qg, kg, cb, sb, S, layer):
    T = xn.shape[0]
    nst = S // PROJ_TM
    table = pl.BlockSpec((PROJ_TM, HEAD_DIM), lambda i: (i % nst, 0))
    vec = pl.BlockSpec((1, HEAD_DIM), lambda i: (0, 0))
    return pl.pallas_call(
        _proj_gqa_kernel,
        grid=(T // PROJ_TM,),
        in_specs=[_token_tile(D_MODEL), _weight_group(layer, 1), vec, vec, table, table],
        out_specs=_token_tile(PROJ_GROUP),
        out_shape=jax.ShapeDtypeStruct((T, PROJ_GROUP), BF16),
        scratch_shapes=[pltpu.VMEM((D_MODEL, PROJ_GROUP), BF16), pltpu.VMEM((PROJ_TM, PROJ_GROUP), F32)],
        compiler_params=_params("arbitrary"),
        name="proj_gqa",
    )(xn, w, qg.reshape(1, HEAD_DIM), kg.reshape(1, HEAD_DIM), cb, sb)


def _proj_diff(xn, w, cc, sca, scb, S, layer):
    T = xn.shape[0]
    nst = S // PROJ_TM
    table = pl.BlockSpec((PROJ_TM, HEAD_DIM), lambda i: (i % nst, 0))
    return pl.pallas_call(
        _proj_diff_kernel,
        grid=(T // PROJ_TM,),
        in_specs=[_token_tile(D_MODEL), _weight_group(layer, 2), table, table, table],
        out_specs=_token_tile(PROJ_GROUP),
        out_shape=jax.ShapeDtypeStruct((T, PROJ_GROUP), BF16),
        scratch_shapes=[pltpu.VMEM((D_MODEL, PROJ_GROUP), BF16)],
        compiler_params=_params("arbitrary"),
        name="proj_diff",
    )(xn, w, cc, sca, scb)


def _transpose_values(v_ref, vt_ref):
    S = v_ref.shape[0]
    step = 512
    for c in range(S // step):
        blk = v_ref[c * step:(c + 1) * step, :].astype(F32)
        vt_ref[:, c * step:(c + 1) * step] = blk.T.astype(BF16)


def _attention_rows(q_ref, k_ref, vt_ref, s_refs, emit):
    npair = q_ref.shape[0] // (2 * ATT_RB)
    S = k_ref.shape[0]

    def scores(r, slot):
        off = pl.multiple_of(r * ATT_RB, ATT_RB)
        st = lax.dot_general(k_ref[...], q_ref[pl.ds(off, ATT_RB), :], (((1,), (1,)), ((), ())),
                             preferred_element_type=F32)
        s_refs[slot][...] = st
        return jnp.max(st, axis=0, keepdims=True)

    def consume(slot, m):
        acc = jnp.zeros((HEAD_DIM, ATT_RB), F32)
        l = jnp.zeros((1, ATT_RB), F32)
        for c in range(S // ATT_KC):
            p = jnp.exp2(s_refs[slot][c * ATT_KC:(c + 1) * ATT_KC, :] - m)
            l = l + jnp.sum(p, axis=0, keepdims=True)
            acc = acc + jnp.dot(vt_ref[:, c * ATT_KC:(c + 1) * ATT_KC], p.astype(BF16),
                                preferred_element_type=F32)
        return (acc / l).T

    def body(i, m_even):
        m_odd = scores(2 * i + 1, 1)
        o_even = consume(0, m_even)
        m_next = scores(2 * i + 2, 0)
        o_odd = consume(1, m_odd)
        emit(i, o_even, o_odd)
        return m_next

    m_even = lax.fori_loop(0, npair - 1, body, scores(0, 0))
    m_odd = scores(2 * npair - 1, 1)
    o_even = consume(0, m_even)
    o_odd = consume(1, m_odd)
    emit(npair - 1, o_even, o_odd)


def _gqa_kernel(q_ref, k_ref, v_ref, o_ref, vt_ref, s0_ref, s1_ref, qs_ref, os_ref):
    S = k_ref.shape[0]
    _transpose_values(v_ref, vt_ref)
    for g in range(GQA_GROUP):
        qs_ref[g * S:(g + 1) * S, :] = q_ref[:, g * HEAD_DIM:(g + 1) * HEAD_DIM]

    def emit(i, o_even, o_odd):
        off = pl.multiple_of(i * 2 * ATT_RB, 2 * ATT_RB)
        os_ref[pl.ds(off, ATT_RB), :] = o_even.astype(os_ref.dtype)
        os_ref[pl.ds(off + ATT_RB, ATT_RB), :] = o_odd.astype(os_ref.dtype)

    _attention_rows(qs_ref, k_ref, vt_ref, (s0_ref, s1_ref), emit)
    for g in range(GQA_GROUP):
        o_ref[:, g * HEAD_DIM:(g + 1) * HEAD_DIM] = os_ref[g * S:(g + 1) * S, :]


def _gqa(proj, B, S):
    T = proj.shape[0]
    gw = GQA_GROUP * HEAD_DIM
    q0 = 0
    k0 = GQA_HEADS
    v0 = k0 + GQA_KV
    return pl.pallas_call(
        _gqa_kernel,
        grid=(B, GQA_KV),
        in_specs=[
            pl.BlockSpec((S, gw), lambda b, kh: (b, q0 + kh)),
            pl.BlockSpec((S, HEAD_DIM), lambda b, kh: (b, k0 + kh)),
            pl.BlockSpec((S, HEAD_DIM), lambda b, kh: (b, v0 + kh)),
        ],
        out_specs=pl.BlockSpec((S, gw), lambda b, kh: (b, kh)),
        out_shape=jax.ShapeDtypeStruct((T, GQA_HEADS * HEAD_DIM), BF16),
        scratch_shapes=[pltpu.VMEM((HEAD_DIM, S), BF16),
                        pltpu.VMEM((S, ATT_RB), F32), pltpu.VMEM((S, ATT_RB), F32),
                        pltpu.VMEM((GQA_GROUP * S, HEAD_DIM), BF16),
                        pltpu.VMEM((GQA_GROUP * S, HEAD_DIM), BF16)],
        compiler_params=_params("parallel", "parallel"),
        name="gqa",
    )(proj, proj, proj)


def _diff_kernel(lam_ref, gain_ref, q_ref, k_ref, v_ref, o_ref, vt_ref, s0_ref, s1_ref, qc_ref, *, lam_init):
    _transpose_values(v_ref, vt_ref)

    S = q_ref.shape[0]
    lane = lax.broadcasted_iota(jnp.int32, (ATT_RB, HEAD_DIM), 1)
    first = lane < DIFF_QK_DIM
    for r in range(S // ATT_RB):
        q = q_ref[r * ATT_RB:(r + 1) * ATT_RB, :]
        zero = jnp.zeros_like(q)
        qc_ref[2 * r * ATT_RB:(2 * r + 1) * ATT_RB, :] = jnp.where(first, q, zero)
        qc_ref[(2 * r + 1) * ATT_RB:(2 * r + 2) * ATT_RB, :] = jnp.where(first, zero, q)

    lp = lam_ref[...]
    lam = (jnp.exp(jnp.sum(lp[0:1] * lp[1:2], axis=-1, keepdims=True))
           - jnp.exp(jnp.sum(lp[2:3] * lp[3:4], axis=-1, keepdims=True)) + lam_init)

    def emit(i, o1, o2):
        off = pl.multiple_of(i * ATT_RB, ATT_RB)
        o = o1 - lam * o2
        o = _rms_rows(o, gain_ref[...]) * (1.0 - lam_init)
        o_ref[pl.ds(off, ATT_RB), :] = o.astype(o_ref.dtype)

    _attention_rows(qc_ref, k_ref, vt_ref, (s0_ref, s1_ref), emit)


def _diff(proj, lam_params, gain, layer, B, S):
    T = proj.shape[0]
    q0 = 0
    k0 = DIFF_HEADS
    v0 = 2 * DIFF_HEADS
    lam_init = 0.8 - 0.6 * math.exp(-0.3 * layer)
    return pl.pallas_call(
        functools.partial(_diff_kernel, lam_init=lam_init),
        grid=(B, DIFF_HEADS),
        in_specs=[
            pl.BlockSpec((4, DIFF_QK_DIM), lambda b, h: (0, 0)),
            pl.BlockSpec((1, HEAD_DIM), lambda b, h: (0, 0)),
            pl.BlockSpec((S, HEAD_DIM), lambda b, h: (b, q0 + h)),
            pl.BlockSpec((S, HEAD_DIM), lambda b, h: (b, k0 + h)),
            pl.BlockSpec((S, HEAD_DIM), lambda b, h: (b, v0 + h)),
        ],
        out_specs=pl.BlockSpec((S, HEAD_DIM), lambda b, h: (b, h)),
        out_shape=jax.ShapeDtypeStruct((T, DIFF_HEADS * HEAD_DIM), BF16),
        scratch_shapes=[pltpu.VMEM((HEAD_DIM, S), BF16),
                        pltpu.VMEM((S, ATT_RB), F32), pltpu.VMEM((S, ATT_RB), F32),
                        pltpu.VMEM((2 * S, HEAD_DIM), BF16)],
        compiler_params=_params("parallel", "parallel"),
        name="diff",
    )(lam_params, gain.reshape(1, HEAD_DIM), proj, proj, proj)


def _na_window(blk, rows):
    r0 = blk * NA_QROWS
    return r0, int(np.clip(r0 - NA_KH // 2, 0, rows - NA_KROWS))


def _na_slab_index(r0, ws, k, j, rows):
    n_dr = 2 * NA_KH - 1
    kr = ws + k
    a, valid = [], []
    for r in (r0 + 2 * j, r0 + 2 * j + 1):
        rs = int(np.clip(r - NA_KH // 2, 0, rows - NA_KH))
        valid.append(rs <= kr < rs + NA_KH)
        a.append(kr - r + (NA_KH - 1))
    if valid[0] and valid[1]:
        return a[0] - 1
    if valid[0]:
        return (n_dr - 1) + a[0]
    if valid[1]:
        return (n_dr - 1) + n_dr + a[1]
    return None


def _na_kernel(q_ref, k_ref, v_ref, slab_ref, o_ref, vt_ref, *, rows):
    _transpose_values(v_ref, vt_ref)
    tq = NA_QROWS * GRID_W
    tk = NA_KROWS * GRID_W
    masked = jnp.full((GRID_W, 2 * GRID_W), NEG_BIG, F32)
    for blk in range(rows // NA_QROWS):
        r0, ws = _na_window(blk, rows)
        bias_rows = []
        for k in range(NA_KROWS):
            slabs = []
            for j in range(NA_QROWS // 2):
                idx = _na_slab_index(r0, ws, k, j, rows)
                slabs.append(masked if idx is None else slab_ref[idx])
            bias_rows.append(jnp.concatenate(slabs, axis=1))
        bias = jnp.concatenate(bias_rows, axis=0)
        q = q_ref[r0 * GRID_W:r0 * GRID_W + tq, :]
        kw = k_ref[ws * GRID_W:ws * GRID_W + tk, :]
        st = lax.dot_general(kw, q, (((1,), (1,)), ((), ())), preferred_element_type=F32) + bias
        m = jnp.max(st, axis=0, keepdims=True)
        p = jnp.exp2(st - m)
        l = jnp.sum(p, axis=0, keepdims=True)
        ot = jnp.dot(vt_ref[:, ws * GRID_W:ws * GRID_W + tk], p.astype(BF16), preferred_element_type=F32)
        o_ref[r0 * GRID_W:r0 * GRID_W + tq, :] = (ot / l).T.astype(o_ref.dtype)


def _na(proj, slabs, B, S, layer):
    T = proj.shape[0]
    rows = S // GRID_W
    n_slab = slabs.shape[2]
    return pl.pallas_call(
        functools.partial(_na_kernel, rows=rows),
        grid=(B, NA_HEADS),
        in_specs=[
            pl.BlockSpec((S, HEAD_DIM), lambda b, h: (b, h)),
            pl.BlockSpec((S, HEAD_DIM), lambda b, h: (b, NA_HEADS + h)),
            pl.BlockSpec((S, HEAD_DIM), lambda b, h: (b, 2 * NA_HEADS + h)),
            pl.BlockSpec((None, None, n_slab, GRID_W, 2 * GRID_W), lambda b, h: (layer, h, 0, 0, 0)),
        ],
        out_specs=pl.BlockSpec((S, HEAD_DIM), lambda b, h: (b, h)),
        out_shape=jax.ShapeDtypeStruct((T, NA_HEADS * HEAD_DIM), BF16),
        scratch_shapes=[pltpu.VMEM((HEAD_DIM, S), BF16)],
        compiler_params=_params("parallel", "parallel"),
        name="na",
    )(proj, proj, proj, slabs)


def _na_bias_slabs(rpb):
    n_dc = 2 * NA_KW - 1
    cq = np.arange(GRID_W)
    cs = np.clip(cq - NA_KW // 2, 0, GRID_W - NA_KW)
    ck = np.arange(GRID_W)
    valid_c = (ck[:, None] >= cs[None, :]) & (ck[:, None] < cs[None, :] + NA_KW)
    dc = ck[:, None] - cq[None, :] + (NA_KW - 1)
    onehot = ((dc[None] == np.arange(n_dc)[:, None, None]) & valid_c[None]).astype(np.float32)
    t = jnp.einsum('lhad,dkq->lhakq', rpb, jnp.asarray(onehot), precision=lax.Precision.HIGHEST)
    t = (t + jnp.asarray(np.where(valid_c, 0.0, NEG_BIG).astype(np.float32))) * LOG2E
    masked = jnp.full_like(t, NEG_BIG)
    both = jnp.concatenate([t[:, :, 1:], t[:, :, :-1]], axis=-1)
    left = jnp.concatenate([t, masked], axis=-1)
    right = jnp.concatenate([masked, t], axis=-1)
    return jnp.concatenate([both, left, right], axis=2)


def _outproj_kernel(h_ref, a_ref, b_ref, c_ref, w_ref, o_ref, w_bf_ref):
    @pl.when(pl.program_id(0) == 0)
    def _():
        w_bf_ref[...] = w_ref[...].astype(BF16)

    na = a_ref.shape[1]
    nb = b_ref.shape[1]
    acc = jnp.dot(a_ref[...], w_bf_ref[0:na, :], preferred_element_type=F32)
    acc += jnp.dot(b_ref[...], w_bf_ref[na:na + nb, :], preferred_element_type=F32)
    acc += jnp.dot(c_ref[...], w_bf_ref[na + nb:, :], preferred_element_type=F32)
    o_ref[...] = h_ref[...] + acc


def _outproj(h, oa, ob, oc, w, layer):
    T = h.shape[0]
    return pl.pallas_call(
        _outproj_kernel,
        grid=(T // OUT_TM,),
        in_specs=[
            pl.BlockSpec((OUT_TM, D_MODEL), lambda i: (i, 0)),
            pl.BlockSpec((OUT_TM, oa.shape[1]), lambda i: (i, 0)),
            pl.BlockSpec((OUT_TM, ob.shape[1]), lambda i: (i, 0)),
            pl.BlockSpec((OUT_TM, oc.shape[1]), lambda i: (i, 0)),
            pl.BlockSpec((None, D_MODEL, D_MODEL), lambda i: (layer, 0, 0), pipeline_mode=pl.Buffered(1)),
        ],
        out_specs=pl.BlockSpec((OUT_TM, D_MODEL), lambda i: (i, 0)),
        out_shape=jax.ShapeDtypeStruct((T, D_MODEL), F32),
        scratch_shapes=[pltpu.VMEM((D_MODEL, D_MODEL), BF16)],
        compiler_params=_params("arbitrary"),
        name="outproj",
    )(h, oa, ob, oc, w)


def _ple_kernel(h_ref, g_ref, wg_ref, p_ref, wp_ref, gf_ref, o_ref, wg_bf_ref, wp_bf_ref, *, final):
    @pl.when(pl.program_id(0) == 0)
    def _():
        wg_bf_ref[...] = wg_ref[...].astype(BF16)
        wp_bf_ref[...] = wp_ref[...].astype(BF16)

    x = h_ref[...]
    xn = _rms_rows(x, g_ref[...]).astype(BF16)
    gate = jax.nn.sigmoid(jnp.dot(xn, wg_bf_ref[...], preferred_element_type=F32))
    emb = jnp.dot(p_ref[...].astype(BF16), wp_bf_ref[...], preferred_element_type=F32)
    y = x + gate * emb
    if final:
        y = _rms_rows(y, gf_ref[...])
    o_ref[...] = y


def _ple(h, g, wg, p, wp, g_final, layer, final):
    T = h.shape[0]
    resident = pl.Buffered(1)
    return pl.pallas_call(
        functools.partial(_ple_kernel, final=final),
        grid=(T // PLE_TM,),
        in_specs=[
            pl.BlockSpec((PLE_TM, D_MODEL), lambda i: (i, 0)),
            pl.BlockSpec((1, D_MODEL), lambda i: (0, 0)),
            pl.BlockSpec((None, D_MODEL, D_MODEL), lambda i: (layer, 0, 0), pipeline_mode=resident),
            pl.BlockSpec((None, PLE_TM, PLE_DIM), lambda i: (layer, i, 0)),
            pl.BlockSpec((None, PLE_DIM, D_MODEL), lambda i: (layer, 0, 0), pipeline_mode=resident),
            pl.BlockSpec((1, D_MODEL), lambda i: (0, 0)),
        ],
        out_specs=pl.BlockSpec((PLE_TM, D_MODEL), lambda i: (i, 0)),
        out_shape=jax.ShapeDtypeStruct((T, D_MODEL), F32),
        scratch_shapes=[pltpu.VMEM((D_MODEL, D_MODEL), BF16), pltpu.VMEM((PLE_DIM, D_MODEL), BF16)],
        compiler_params=_params("arbitrary"),
        name="ple",
    )(h, g.reshape(1, D_MODEL), wg, p, wp, g_final.reshape(1, D_MODEL))


def _rope_tables(S):
    t = jnp.arange(S)
    inv_ax = jnp.power(ROPE_THETA, -jnp.arange(0, AXIAL_DIM, 2, dtype=F32) / AXIAL_DIM)
    ang_b = jnp.concatenate([(t // GRID_W).astype(F32)[:, None] * inv_ax,
                             (t % GRID_W).astype(F32)[:, None] * inv_ax], axis=-1)
    cos_b, sin_b = jnp.cos(ang_b), jnp.sin(ang_b)
    cb = jnp.concatenate([cos_b, cos_b], axis=-1)
    sb = jnp.concatenate([-sin_b, sin_b], axis=-1)
    inv_1d = jnp.power(ROPE_THETA, -jnp.arange(0, DIFF_QK_DIM, 2, dtype=F32) / DIFF_QK_DIM)
    ang_c = t.astype(F32)[:, None] * inv_1d
    cos_c, sin_c = jnp.cos(ang_c), jnp.sin(ang_c)
    zero = jnp.zeros_like(sin_c)
    cc = jnp.concatenate([cos_c] * 4, axis=-1)
    sca = jnp.concatenate([-sin_c, zero, -sin_c, zero], axis=-1)
    scb = jnp.concatenate([zero, sin_c, zero, sin_c], axis=-1)
    return cb, sb, cc, sca, scb


def kernel(x, p, g_ffn1, w1_gate, w1_up, w1_down, g_mix, w_in, na_rpb, gqa_q_gain, gqa_k_gain,
           diff_lambda, diff_subln_gain, w_out, g_ffn2, w2_gate, w2_up, w2_down, g_ple,
           w_ple_gate, w_ple_proj, g_final):
    B, S, D = x.shape
    depth = w_in.shape[0]
    T = B * S
    cb, sb, cc, sca, scb = _rope_tables(S)
    slabs = _na_bias_slabs(na_rpb)
    p_flat = p.reshape(depth, T, PLE_DIM)
    h = x.reshape(T, D)
    for i in range(depth):
        h = _ffn(h, g_ffn1[i], w1_gate, w1_up, w1_down, i)
        proj_a, xn = _proj_na(h, g_mix[i], w_in, i)
        q_gain = gqa_q_gain[i] * (HEAD_DIM ** -0.5 * LOG2E)
        proj_b = _proj_gqa(xn, w_in, q_gain, gqa_k_gain[i], cb, sb, S, i)
        proj_c = _proj_diff(xn, w_in, cc, sca, scb, S, i)
        out_a = _na(proj_a, slabs, B, S, i)
        out_b = _gqa(proj_b, B, S)
        out_c = _diff(proj_c, diff_lambda[i], diff_subln_gain[i], i, B, S)
        h = _outproj(h, out_a, out_b, out_c, w_out, i)
        h = _ffn(h, g_ffn2[i], w2_gate, w2_up, w2_down, i)
        h = _ple(h, g_ple[i], w_ple_gate, p_flat, w_ple_proj, g_final, i, final=(i == depth - 1))
    return h.reshape(B, S, D)
```

```python
import functools
import math

import numpy as np
import jax
import jax.numpy as jnp
from jax import lax
from jax.experimental import pallas as pl
from jax.experimental.pallas import tpu as pltpu

F32 = jnp.float32
BF16 = jnp.bfloat16

D_MODEL = 2048
EPS = 1e-6
GRID_W = 64
HEAD_DIM = 128
NA_HEADS = 4
NA_KH = 8
NA_KW = 16
GQA_HEADS = 8
GQA_KV = 2
GQA_GROUP = GQA_HEADS // GQA_KV
AXIAL_DIM = HEAD_DIM // 2
DIFF_HEADS = 4
DIFF_QK_DIM = HEAD_DIM // 2
IN_COLS = 4608
D_FF = 5632
PLE_DIM = 256
ROPE_THETA = 10000.0

NEG_BIG = -1e30
LOG2E = math.log2(math.e)
VMEM_LIMIT_BYTES = 56 * 1024 * 1024
VMEM_LIMIT_LARGE_BYTES = 60 * 1024 * 1024

FFN_TM = 1024
FFN_TF = 512
PROJ_TM = 512
OUT_TM = 512
PLE_TM = 512
ATT_RB = 512
ATT_KC = 512
GQA_UNROLL = 2
DIFF_UNROLL = 4
NA_QROWS = 8
NA_KROWS = 16


def _params(*sem, vmem_limit_bytes=VMEM_LIMIT_BYTES):
    return pltpu.CompilerParams(dimension_semantics=sem, vmem_limit_bytes=vmem_limit_bytes)


def _rms_rows(x, g):
    return x * lax.rsqrt(jnp.mean(x * x, axis=-1, keepdims=True) + EPS) * g


def _ffn_kernel(h_ref, g_ref, wg_ref, wu_ref, wd_ref, o_ref, xn_ref):
    @pl.when(pl.program_id(1) == 0)
    def _():
        x = h_ref[...]
        xn_ref[...] = _rms_rows(x, g_ref[...]).astype(BF16)
        o_ref[...] = x

    xn = xn_ref[...]
    a = jnp.dot(xn, wg_ref[...].astype(BF16), preferred_element_type=F32)
    u = jnp.dot(xn, wu_ref[...].astype(BF16), preferred_element_type=F32)
    act = (a * jax.nn.sigmoid(a) * (0.5 * u)).astype(BF16)
    o_ref[...] += jnp.dot(act, wd_ref[...].astype(BF16), preferred_element_type=F32)


def _ffn(h, g, wg, wu, wd, layer):
    T = h.shape[0]
    return pl.pallas_call(
        _ffn_kernel,
        grid=(T // FFN_TM, D_FF // FFN_TF),
        in_specs=[
            pl.BlockSpec((FFN_TM, D_MODEL), lambda i, j: (i, 0), pipeline_mode=pl.Buffered(1)),
            pl.BlockSpec((1, D_MODEL), lambda i, j: (0, 0)),
            pl.BlockSpec((None, D_MODEL, FFN_TF), lambda i, j: (layer, 0, j)),
            pl.BlockSpec((None, D_MODEL, FFN_TF), lambda i, j: (layer, 0, j)),
            pl.BlockSpec((None, FFN_TF, D_MODEL), lambda i, j: (layer, j, 0)),
        ],
        out_specs=pl.BlockSpec((FFN_TM, D_MODEL), lambda i, j: (i, 0)),
        out_shape=jax.ShapeDtypeStruct((T, D_MODEL), F32),
        scratch_shapes=[pltpu.VMEM((FFN_TM, D_MODEL), BF16)],
        compiler_params=_params("parallel", "arbitrary", vmem_limit_bytes=VMEM_LIMIT_LARGE_BYTES),
        name="ffn",
    )(h, g.reshape(1, D_MODEL), wg, wu, wd)


PROJ_GROUP = IN_COLS // 3


def _round_weight_once(w_ref, w_bf_ref):
    @pl.when(pl.program_id(0) == 0)
    def _():
        w_bf_ref[...] = w_ref[...].astype(BF16)


def _proj_na_kernel(h_ref, g_ref, w_ref, o_ref, xn_ref, w_bf_ref):
    _round_weight_once(w_ref, w_bf_ref)
    xn = _rms_rows(h_ref[...], g_ref[...]).astype(BF16)
    xn_ref[...] = xn
    y = jnp.dot(xn, w_bf_ref[...], preferred_element_type=F32)
    nq = NA_HEADS * HEAD_DIM
    o_ref[:, :nq] = (y[:, :nq] * (HEAD_DIM ** -0.5 * LOG2E)).astype(BF16)
    o_ref[:, nq:] = y[:, nq:].astype(BF16)


def _proj_gqa_kernel(xn_ref, w_ref, qg_ref, kg_ref, cb_ref, sb_ref, o_ref, w_bf_ref, y_ref):
    _round_weight_once(w_ref, w_bf_ref)
    y_ref[...] = jnp.dot(xn_ref[...], w_bf_ref[...], preferred_element_type=F32)
    n_roped = GQA_HEADS + GQA_KV

    @pl.when(pl.program_id(0) < pl.num_programs(0))
    def _():
        for k in range(n_roped):
            gain = qg_ref[...] if k < GQA_HEADS else kg_ref[...]
            xg = _rms_rows(y_ref[:, k * HEAD_DIM:(k + 1) * HEAD_DIM], gain)
            r_out = xg * cb_ref[...] + pltpu.roll(xg, HEAD_DIM // 2, 1) * sb_ref[...]
            o_ref[:, k * HEAD_DIM:(k + 1) * HEAD_DIM] = r_out.astype(BF16)
        o_ref[:, n_roped * HEAD_DIM:] = y_ref[:, n_roped * HEAD_DIM:].astype(BF16)


def _proj_diff_kernel(xn_ref, w_ref, cc_ref, sca_ref, scb_ref, o_ref, w_bf_ref):
    _round_weight_once(w_ref, w_bf_ref)
    y = jnp.dot(xn_ref[...], w_bf_ref[...], preferred_element_type=F32)
    for k in range(2 * DIFF_HEADS):
        x = y[:, k * HEAD_DIM:(k + 1) * HEAD_DIM]
        r = (x * cc_ref[...] + pltpu.roll(x, HEAD_DIM - DIFF_QK_DIM // 2, 1) * sca_ref[...]
             + pltpu.roll(x, DIFF_QK_DIM // 2, 1) * scb_ref[...])
        if k < DIFF_HEADS:
            r = r * (DIFF_QK_DIM ** -0.5 * LOG2E)
        o_ref[:, k * HEAD_DIM:(k + 1) * HEAD_DIM] = r.astype(BF16)
    o_ref[:, 2 * DIFF_HEADS * HEAD_DIM:] = y[:, 2 * DIFF_HEADS * HEAD_DIM:].astype(BF16)


def _token_tile(width):
    return pl.BlockSpec((PROJ_TM, width), lambda i: (i, 0))


def _weight_group(layer, group):
    return pl.BlockSpec((None, D_MODEL, PROJ_GROUP), lambda i: (layer, 0, group), pipeline_mode=pl.Buffered(1))


def _proj_na(h, g, w, layer):
    T = h.shape[0]
    return pl.pallas_call(
        _proj_na_kernel,
        grid=(T // PROJ_TM,),
        in_specs=[_token_tile(D_MODEL), pl.BlockSpec((1, D_MODEL), lambda i: (0, 0)), _weight_group(layer, 0)],
        out_specs=[_token_tile(PROJ_GROUP), _token_tile(D_MODEL)],
        out_shape=[jax.ShapeDtypeStruct((T, PROJ_GROUP), BF16), jax.ShapeDtypeStruct((T, D_MODEL), BF16)],
        scratch_shapes=[pltpu.VMEM((D_MODEL, PROJ_GROUP), BF16)],
        compiler_params=_params("arbitrary"),
        name="proj_na",
    )(h, g.reshape(1, D_MODEL), w)


def _proj_gqa(xn, w, qg, kg, cb, sb, S, layer):
    T = xn.shape[0]
    nst = S // PROJ_TM
    table = pl.BlockSpec((PROJ_TM, HEAD_DIM), lambda i: (i % nst, 0))
    vec = pl.BlockSpec((1, HEAD_DIM), lambda i: (0, 0))
    return pl.pallas_call(
        _proj_gqa_kernel,
        grid=(T // PROJ_TM,),
        in_specs=[_token_tile(D_MODEL), _weight_group(layer, 1), vec, vec, table, table],
        out_specs=_token_tile(PROJ_GROUP),
        out_shape=jax.ShapeDtypeStruct((T, PROJ_GROUP), BF16),
        scratch_shapes=[pltpu.VMEM((D_MODEL, PROJ_GROUP), BF16), pltpu.VMEM((PROJ_TM, PROJ_GROUP), F32)],
        compiler_params=_params("arbitrary"),
        name="proj_gqa",
    )(xn, w, qg.reshape(1, HEAD_DIM), kg.reshape(1, HEAD_DIM), cb, sb)


def _proj_diff(xn, w, cc, sca, scb, S, layer):
    T = xn.shape[0]
    nst = S // PROJ_TM
    table = pl.BlockSpec((PROJ_TM, HEAD_DIM), lambda i: (i % nst, 0))
    return pl.pallas_call(
        _proj_diff_kernel,
        grid=(T // PROJ_TM,),
        in_specs=[_token_tile(D_MODEL), _weight_group(layer, 2), table, table, table],
        out_specs=_token_tile(PROJ_GROUP),
        out_shape=jax.ShapeDtypeStruct((T, PROJ_GROUP), BF16),
        scratch_shapes=[pltpu.VMEM((D_MODEL, PROJ_GROUP), BF16)],
        compiler_params=_params("arbitrary"),
        name="proj_diff",
    )(xn, w, cc, sca, scb)


def _transpose_values(v_ref, vt_ref):
    S = v_ref.shape[0]
    step = 512
    for c in range(S // step):
        blk = v_ref[c * step:(c + 1) * step, :].astype(F32)
        vt_ref[:, c * step:(c + 1) * step] = blk.T.astype(BF16)


def _attention_rows(q_ref, k_ref, vt_ref, s_refs, emit, unroll):
    npair = q_ref.shape[0] // (2 * ATT_RB)
    S = k_ref.shape[0]

    def scores(r, slot):
        off = pl.multiple_of(r * ATT_RB, ATT_RB)
        st = lax.dot_general(k_ref[...], q_ref[pl.ds(off, ATT_RB), :], (((1,), (1,)), ((), ())),
                             preferred_element_type=F32)
        s_refs[slot][...] = st
        return jnp.max(st, axis=0, keepdims=True)

    def consume(slot, m):
        acc = jnp.zeros((HEAD_DIM, ATT_RB), F32)
        l = jnp.zeros((1, ATT_RB), F32)
        for c in range(S // ATT_KC):
            p = jnp.exp2(s_refs[slot][c * ATT_KC:(c + 1) * ATT_KC, :] - m)
            l = l + jnp.sum(p, axis=0, keepdims=True)
            acc = acc + jnp.dot(vt_ref[:, c * ATT_KC:(c + 1) * ATT_KC], p.astype(BF16),
                                preferred_element_type=F32)
        return (acc / l).T

    def pairs(first_pair, m, last):
        for u in range(unroll // 2):
            pair = first_pair + u
            m_odd = scores(2 * pair + 1, 1)
            o_even = consume(0, m)
            if not (last and u == unroll // 2 - 1):
                m = scores(2 * pair + 2, 0)
            o_odd = consume(1, m_odd)
            emit(pair, o_even, o_odd)
        return m

    n_body = npair // (unroll // 2)
    m = lax.fori_loop(0, n_body - 1, lambda i, m: pairs(i * (unroll // 2), m, False), scores(0, 0))
    pairs((n_body - 1) * (unroll // 2), m, True)


def _gqa_kernel(q_ref, k_ref, v_ref, o_ref, vt_ref, s0_ref, s1_ref, qs_ref, os_ref):
    S = k_ref.shape[0]
    _transpose_values(v_ref, vt_ref)
    for g in range(GQA_GROUP):
        qs_ref[g * S:(g + 1) * S, :] = q_ref[:, g * HEAD_DIM:(g + 1) * HEAD_DIM]

    def emit(i, o_even, o_odd):
        off = pl.multiple_of(i * 2 * ATT_RB, 2 * ATT_RB)
        os_ref[pl.ds(off, ATT_RB), :] = o_even.astype(os_ref.dtype)
        os_ref[pl.ds(off + ATT_RB, ATT_RB), :] = o_odd.astype(os_ref.dtype)

    _attention_rows(qs_ref, k_ref, vt_ref, (s0_ref, s1_ref), emit, GQA_UNROLL)
    for g in range(GQA_GROUP):
        o_ref[:, g * HEAD_DIM:(g + 1) * HEAD_DIM] = os_ref[g * S:(g + 1) * S, :]


def _gqa(proj, B, S):
    T = proj.shape[0]
    gw = GQA_GROUP * HEAD_DIM
    q0 = 0
    k0 = GQA_HEADS
    v0 = k0 + GQA_KV
    return pl.pallas_call(
        _gqa_kernel,
        grid=(B, GQA_KV),
        in_specs=[
            pl.BlockSpec((S, gw), lambda b, kh: (b, q0 + kh)),
            pl.BlockSpec((S, HEAD_DIM), lambda b, kh: (b, k0 + kh)),
            pl.BlockSpec((S, HEAD_DIM), lambda b, kh: (b, v0 + kh)),
        ],
        out_specs=pl.BlockSpec((S, gw), lambda b, kh: (b, kh)),
        out_shape=jax.ShapeDtypeStruct((T, GQA_HEADS * HEAD_DIM), BF16),
        scratch_shapes=[pltpu.VMEM((HEAD_DIM, S), BF16),
                        pltpu.VMEM((S, ATT_RB), F32), pltpu.VMEM((S, ATT_RB), F32),
                        pltpu.VMEM((GQA_GROUP * S, HEAD_DIM), BF16),
                        pltpu.VMEM((GQA_GROUP * S, HEAD_DIM), BF16)],
        compiler_params=_params("parallel", "parallel", vmem_limit_bytes=VMEM_LIMIT_LARGE_BYTES),
        name="gqa",
    )(proj, proj, proj)


def _diff_kernel(lam_ref, gain_ref, q_ref, k_ref, v_ref, o_ref, vt_ref, s0_ref, s1_ref, qc_ref, *, lam_init):
    _transpose_values(v_ref, vt_ref)

    S = q_ref.shape[0]
    lane = lax.broadcasted_iota(jnp.int32, (ATT_RB, HEAD_DIM), 1)
    first = lane < DIFF_QK_DIM
    for r in range(S // ATT_RB):
        q = q_ref[r * ATT_RB:(r + 1) * ATT_RB, :]
        zero = jnp.zeros_like(q)
        qc_ref[2 * r * ATT_RB:(2 * r + 1) * ATT_RB, :] = jnp.where(first, q, zero)
        qc_ref[(2 * r + 1) * ATT_RB:(2 * r + 2) * ATT_RB, :] = jnp.where(first, zero, q)

    lp = lam_ref[...]
    lam = (jnp.exp(jnp.sum(lp[0:1] * lp[1:2], axis=-1, keepdims=True))
           - jnp.exp(jnp.sum(lp[2:3] * lp[3:4], axis=-1, keepdims=True)) + lam_init)

    def emit(i, o1, o2):
        off = pl.multiple_of(i * ATT_RB, ATT_RB)
        o = o1 - lam * o2
        o = _rms_rows(o, gain_ref[...]) * (1.0 - lam_init)
        o_ref[pl.ds(off, ATT_RB), :] = o.astype(o_ref.dtype)

    _attention_rows(qc_ref, k_ref, vt_ref, (s0_ref, s1_ref), emit, DIFF_UNROLL)


def _diff(proj, lam_params, gain, layer, B, S):
    T = proj.shape[0]
    q0 = 0
    k0 = DIFF_HEADS
    v0 = 2 * DIFF_HEADS
    lam_init = 0.8 - 0.6 * math.exp(-0.3 * layer)
    return pl.pallas_call(
        functools.partial(_diff_kernel, lam_init=lam_init),
        grid=(B, DIFF_HEADS),
        in_specs=[
            pl.BlockSpec((4, DIFF_QK_DIM), lambda b, h: (0, 0)),
            pl.BlockSpec((1, HEAD_DIM), lambda b, h: (0, 0)),
            pl.BlockSpec((S, HEAD_DIM), lambda b, h: (b, q0 + h)),
            pl.BlockSpec((S, HEAD_DIM), lambda b, h: (b, k0 + h)),
            pl.BlockSpec((S, HEAD_DIM), lambda b, h: (b, v0 + h)),
        ],
        out_specs=pl.BlockSpec((S, HEAD_DIM), lambda b, h: (b, h)),
        out_shape=jax.ShapeDtypeStruct((T, DIFF_HEADS * HEAD_DIM), BF16),
        scratch_shapes=[pltpu.VMEM((HEAD_DIM, S), BF16),
                        pltpu.VMEM((S, ATT_RB), F32), pltpu.VMEM((S, ATT_RB), F32),
                        pltpu.VMEM((2 * S, HEAD_DIM), BF16)],
        compiler_params=_params("parallel", "parallel", vmem_limit_bytes=VMEM_LIMIT_LARGE_BYTES),
        name="diff",
    )(lam_params, gain.reshape(1, HEAD_DIM), proj, proj, proj)


def _na_window(blk, rows):
    r0 = blk * NA_QROWS
    return r0, int(np.clip(r0 - NA_KH // 2, 0, rows - NA_KROWS))


def _na_slab_index(r0, ws, k, j, rows):
    n_dr = 2 * NA_KH - 1
    kr = ws + k
    a, valid = [], []
    for r in (r0 + 2 * j, r0 + 2 * j + 1):
        rs = int(np.clip(r - NA_KH // 2, 0, rows - NA_KH))
        valid.append(rs <= kr < rs + NA_KH)
        a.append(kr - r + (NA_KH - 1))
    if valid[0] and valid[1]:
        return a[0] - 1
    if valid[0]:
        return (n_dr - 1) + a[0]
    if valid[1]:
        return (n_dr - 1) + n_dr + a[1]
    return None


def _na_kernel(q_ref, k_ref, v_ref, slab_ref, o_ref, vt_ref, *, rows):
    _transpose_values(v_ref, vt_ref)
    tq = NA_QROWS * GRID_W
    tk = NA_KROWS * GRID_W
    masked = jnp.full((GRID_W, 2 * GRID_W), NEG_BIG, F32)
    for blk in range(rows // NA_QROWS):
        r0, ws = _na_window(blk, rows)
        bias_rows = []
        for k in range(NA_KROWS):
            slabs = []
            for j in range(NA_QROWS // 2):
                idx = _na_slab_index(r0, ws, k, j, rows)
                slabs.append(masked if idx is None else slab_ref[idx])
            bias_rows.append(jnp.concatenate(slabs, axis=1))
        bias = jnp.concatenate(bias_rows, axis=0)
        q = q_ref[r0 * GRID_W:r0 * GRID_W + tq, :]
        kw = k_ref[ws * GRID_W:ws * GRID_W + tk, :]
        st = lax.dot_general(kw, q, (((1,), (1,)), ((), ())), preferred_element_type=F32) + bias
        m = jnp.max(st, axis=0, keepdims=True)
        p = jnp.exp2(st - m)
        l = jnp.sum(p, axis=0, keepdims=True)
        ot = jnp.dot(vt_ref[:, ws * GRID_W:ws * GRID_W + tk], p.astype(BF16), preferred_element_type=F32)
        o_ref[r0 * GRID_W:r0 * GRID_W + tq, :] = (ot / l).T.astype(o_ref.dtype)


def _na(proj, slabs, B, S, layer):
    T = proj.shape[0]
    rows = S // GRID_W
    n_slab = slabs.shape[2]
    return pl.pallas_call(
        functools.partial(_na_kernel, rows=rows),
        grid=(B, NA_HEADS),
        in_specs=[
            pl.BlockSpec((S, HEAD_DIM), lambda b, h: (b, h)),
            pl.BlockSpec((S, HEAD_DIM), lambda b, h: (b, NA_HEADS + h)),
            pl.BlockSpec((S, HEAD_DIM), lambda b, h: (b, 2 * NA_HEADS + h)),
            pl.BlockSpec((None, None, n_slab, GRID_W, 2 * GRID_W), lambda b, h: (layer, h, 0, 0, 0)),
        ],
        out_specs=pl.BlockSpec((S, HEAD_DIM), lambda b, h: (b, h)),
        out_shape=jax.ShapeDtypeStruct((T, NA_HEADS * HEAD_DIM), BF16),
        scratch_shapes=[pltpu.VMEM((HEAD_DIM, S), BF16)],
        compiler_params=_params("parallel", "parallel"),
        name="na",
    )(proj, proj, proj, slabs)


def _na_bias_slabs(rpb):
    n_dc = 2 * NA_KW - 1
    cq = np.arange(GRID_W)
    cs = np.clip(cq - NA_KW // 2, 0, GRID_W - NA_KW)
    ck = np.arange(GRID_W)
    valid_c = (ck[:, None] >= cs[None, :]) & (ck[:, None] < cs[None, :] + NA_KW)
    dc = ck[:, None] - cq[None, :] + (NA_KW - 1)
    onehot = ((dc[None] == np.arange(n_dc)[:, None, None]) & valid_c[None]).astype(np.float32)
    t = jnp.einsum('lhad,dkq->lhakq', rpb, jnp.asarray(onehot), precision=lax.Precision.HIGHEST)
    t = (t + jnp.asarray(np.where(valid_c, 0.0, NEG_BIG).astype(np.float32))) * LOG2E
    masked = jnp.full_like(t, NEG_BIG)
    both = jnp.concatenate([t[:, :, 1:], t[:, :, :-1]], axis=-1)
    left = jnp.concatenate([t, masked], axis=-1)
    right = jnp.concatenate([masked, t], axis=-1)
    return jnp.concatenate([both, left, right], axis=2)


def _outproj_kernel(h_ref, a_ref, b_ref, c_ref, w_ref, o_ref, w_bf_ref):
    @pl.when(pl.program_id(0) == 0)
    def _():
        w_bf_ref[...] = w_ref[...].astype(BF16)

    na = a_ref.shape[1]
    nb = b_ref.shape[1]
    acc = jnp.dot(a_ref[...], w_bf_ref[0:na, :], preferred_element_type=F32)
    acc += jnp.dot(b_ref[...], w_bf_ref[na:na + nb, :], preferred_element_type=F32)
    acc += jnp.dot(c_ref[...], w_bf_ref[na + nb:, :], preferred_element_type=F32)
    o_ref[...] = h_ref[...] + acc


def _outproj(h, oa, ob, oc, w, layer):
    T = h.shape[0]
    return pl.pallas_call(
        _outproj_kernel,
        grid=(T // OUT_TM,),
        in_specs=[
            pl.BlockSpec((OUT_TM, D_MODEL), lambda i: (i, 0)),
            pl.BlockSpec((OUT_TM, oa.shape[1]), lambda i: (i, 0)),
            pl.BlockSpec((OUT_TM, ob.shape[1]), lambda i: (i, 0)),
            pl.BlockSpec((OUT_TM, oc.shape[1]), lambda i: (i, 0)),
            pl.BlockSpec((None, D_MODEL, D_MODEL), lambda i: (layer, 0, 0), pipeline_mode=pl.Buffered(1)),
        ],
        out_specs=pl.BlockSpec((OUT_TM, D_MODEL), lambda i: (i, 0)),
        out_shape=jax.ShapeDtypeStruct((T, D_MODEL), F32),
        scratch_shapes=[pltpu.VMEM((D_MODEL, D_MODEL), BF16)],
        compiler_params=_params("arbitrary"),
        name="outproj",
    )(h, oa, ob, oc, w)


def _ple_kernel(h_ref, g_ref, wg_ref, p_ref, wp_ref, gf_ref, o_ref, wg_bf_ref, wp_bf_ref, *, final):
    @pl.when(pl.program_id(0) == 0)
    def _():
        wg_bf_ref[...] = wg_ref[...].astype(BF16)
        wp_bf_ref[...] = wp_ref[...].astype(BF16)

    x = h_ref[...]
    xn = _rms_rows(x, g_ref[...]).astype(BF16)
    gate = jax.nn.sigmoid(jnp.dot(xn, wg_bf_ref[...], preferred_element_type=F32))
    emb = jnp.dot(p_ref[...].astype(BF16), wp_bf_ref[...], preferred_element_type=F32)
    y = x + gate * emb
    if final:
        y = _rms_rows(y, gf_ref[...])
    o_ref[...] = y


def _ple(h, g, wg, p, wp, g_final, layer, final):
    T = h.shape[0]
    resident = pl.Buffered(1)
    return pl.pallas_call(
        functools.partial(_ple_kernel, final=final),
        grid=(T // PLE_TM,),
        in_specs=[
            pl.BlockSpec((PLE_TM, D_MODEL), lambda i: (i, 0)),
            pl.BlockSpec((1, D_MODEL), lambda i: (0, 0)),
            pl.BlockSpec((None, D_MODEL, D_MODEL), lambda i: (layer, 0, 0), pipeline_mode=resident),
            pl.BlockSpec((None, PLE_TM, PLE_DIM), lambda i: (layer, i, 0)),
            pl.BlockSpec((None, PLE_DIM, D_MODEL), lambda i: (layer, 0, 0), pipeline_mode=resident),
            pl.BlockSpec((1, D_MODEL), lambda i: (0, 0)),
        ],
        out_specs=pl.BlockSpec((PLE_TM, D_MODEL), lambda i: (i, 0)),
        out_shape=jax.ShapeDtypeStruct((T, D_MODEL), F32),
        scratch_shapes=[pltpu.VMEM((D_MODEL, D_MODEL), BF16), pltpu.VMEM((PLE_DIM, D_MODEL), BF16)],
        compiler_params=_params("arbitrary"),
        name="ple",
    )(h, g.reshape(1, D_MODEL), wg, p, wp, g_final.reshape(1, D_MODEL))


def _rope_tables(S):
    t = jnp.arange(S)
    inv_ax = jnp.power(ROPE_THETA, -jnp.arange(0, AXIAL_DIM, 2, dtype=F32) / AXIAL_DIM)
    ang_b = jnp.concatenate([(t // GRID_W).astype(F32)[:, None] * inv_ax,
                             (t % GRID_W).astype(F32)[:, None] * inv_ax], axis=-1)
    cos_b, sin_b = jnp.cos(ang_b), jnp.sin(ang_b)
    cb = jnp.concatenate([cos_b, cos_b], axis=-1)
    sb = jnp.concatenate([-sin_b, sin_b], axis=-1)
    inv_1d = jnp.power(ROPE_THETA, -jnp.arange(0, DIFF_QK_DIM, 2, dtype=F32) / DIFF_QK_DIM)
    ang_c = t.astype(F32)[:, None] * inv_1d
    cos_c, sin_c = jnp.cos(ang_c), jnp.sin(ang_c)
    zero = jnp.zeros_like(sin_c)
    cc = jnp.concatenate([cos_c] * 4, axis=-1)
    sca = jnp.concatenate([-sin_c, zero, -sin_c, zero], axis=-1)
    scb = jnp.concatenate([zero, sin_c, zero, sin_c], axis=-1)
    return cb, sb, cc, sca, scb


def kernel(x, p, g_ffn1, w1_gate, w1_up, w1_down, g_mix, w_in, na_rpb, gqa_q_gain, gqa_k_gain,
           diff_lambda, diff_subln_gain, w_out, g_ffn2, w2_gate, w2_up, w2_down, g_ple,
           w_ple_gate, w_ple_proj, g_final):
    B, S, D = x.shape
    depth = w_in.shape[0]
    T = B * S
    cb, sb, cc, sca, scb = _rope_tables(S)
    slabs = _na_bias_slabs(na_rpb)
    p_flat = p.reshape(depth, T, PLE_DIM)
    h = x.reshape(T, D)
    for i in range(depth):
        h = _ffn(h, g_ffn1[i], w1_gate, w1_up, w1_down, i)
        proj_a, xn = _proj_na(h, g_mix[i], w_in, i)
        q_gain = gqa_q_gain[i] * (HEAD_DIM ** -0.5 * LOG2E)
        proj_b = _proj_gqa(xn, w_in, q_gain, gqa_k_gain[i], cb, sb, S, i)
        proj_c = _proj_diff(xn, w_in, cc, sca, scb, S, i)
        out_a = _na(proj_a, slabs, B, S, i)
        out_b = _gqa(proj_b, B, S)
        out_c = _diff(proj_c, diff_lambda[i], diff_subln_gain[i], i, B, S)
        h = _outproj(h, out_a, out_b, out_c, w_out, i)
        h = _ffn(h, g_ffn2[i], w2_gate, w2_up, w2_down, i)
        h = _ple(h, g_ple[i], w_ple_gate, p_flat, w_ple_proj, g_final, i, final=(i == depth - 1))
    return h.reshape(B, S, D)
```

```python
import functools
import math

import numpy as np
import jax
import jax.numpy as jnp
from jax import lax
from jax.experimental import pallas as pl
from jax.experimental.pallas import tpu as pltpu

F32 = jnp.float32
BF16 = jnp.bfloat16

D_MODEL = 2048
EPS = 1e-6
GRID_W = 64
HEAD_DIM = 128
NA_HEADS = 4
NA_KH = 8
NA_KW = 16
GQA_HEADS = 8
GQA_KV = 2
GQA_GROUP = GQA_HEADS // GQA_KV
AXIAL_DIM = HEAD_DIM // 2
DIFF_HEADS = 4
DIFF_QK_DIM = HEAD_DIM // 2
IN_COLS = 4608
D_FF = 5632
PLE_DIM = 256
ROPE_THETA = 10000.0

NEG_BIG = -1e30
LOG2E = math.log2(math.e)
VMEM_LIMIT_BYTES = 56 * 1024 * 1024
VMEM_LIMIT_LARGE_BYTES = 60 * 1024 * 1024

FFN_TM = 1024
FFN_TF = 512
PROJ_TM = 512
OUT_TM = 512
PLE_TM = 512
ATT_RB = 512
ATT_KC = 512
GQA_UNROLL = 2
DIFF_UNROLL = 4
NA_QROWS = 8
NA_KROWS = 16


def _params(*sem, vmem_limit_bytes=VMEM_LIMIT_BYTES):
    return pltpu.CompilerParams(dimension_semantics=sem, vmem_limit_bytes=vmem_limit_bytes)


def _rms_rows(x, g):
    return x * lax.rsqrt(jnp.mean(x * x, axis=-1, keepdims=True) + EPS) * g


def _ffn_kernel(h_ref, g_ref, wg_ref, wu_ref, wd_ref, o_ref, xn_ref):
    @pl.when(pl.program_id(1) == 0)
    def _():
        x = h_ref[...]
        xn_ref[...] = _rms_rows(x, g_ref[...]).astype(BF16)
        o_ref[...] = x

    xn = xn_ref[...]
    a = jnp.dot(xn, wg_ref[...].astype(BF16), preferred_element_type=F32)
    u = jnp.dot(xn, wu_ref[...].astype(BF16), preferred_element_type=F32)
    act = (a * jax.nn.sigmoid(a) * (0.5 * u)).astype(BF16)
    o_ref[...] += jnp.dot(act, wd_ref[...].astype(BF16), preferred_element_type=F32)


def _ffn(h, g, wg, wu, wd, layer):
    T = h.shape[0]
    return pl.pallas_call(
        _ffn_kernel,
        grid=(T // FFN_TM, D_FF // FFN_TF),
        in_specs=[
            pl.BlockSpec((FFN_TM, D_MODEL), lambda i, j: (i, 0), pipeline_mode=pl.Buffered(1)),
            pl.BlockSpec((1, D_MODEL), lambda i, j: (0, 0)),
            pl.BlockSpec((None, D_MODEL, FFN_TF), lambda i, j: (layer, 0, j)),
            pl.BlockSpec((None, D_MODEL, FFN_TF), lambda i, j: (layer, 0, j)),
            pl.BlockSpec((None, FFN_TF, D_MODEL), lambda i, j: (layer, j, 0)),
        ],
        out_specs=pl.BlockSpec((FFN_TM, D_MODEL), lambda i, j: (i, 0)),
        out_shape=jax.ShapeDtypeStruct((T, D_MODEL), F32),
        scratch_shapes=[pltpu.VMEM((FFN_TM, D_MODEL), BF16)],
        compiler_params=_params("parallel", "arbitrary", vmem_limit_bytes=VMEM_LIMIT_LARGE_BYTES),
        name="ffn",
    )(h, g.reshape(1, D_MODEL), wg, wu, wd)


PROJ_GROUP = IN_COLS // 3


def _round_weight_once(w_ref, w_bf_ref):
    @pl.when(pl.program_id(0) == 0)
    def _():
        w_bf_ref[...] = w_ref[...].astype(BF16)


def _proj_na_kernel(h_ref, g_ref, w_ref, o_ref, xn_ref, w_bf_ref):
    _round_weight_once(w_ref, w_bf_ref)
    xn = _rms_rows(h_ref[...], g_ref[...]).astype(BF16)
    xn_ref[...] = xn
    y = jnp.dot(xn, w_bf_ref[...], preferred_element_type=F32)
    nq = NA_HEADS * HEAD_DIM
    o_ref[:, :nq] = (y[:, :nq] * (HEAD_DIM ** -0.5 * LOG2E)).astype(BF16)
    o_ref[:, nq:] = y[:, nq:].astype(BF16)


def _proj_gqa_kernel(xn_ref, w_ref, qg_ref, kg_ref, cb_ref, sb_ref, o_ref, w_bf_ref, y_ref):
    _round_weight_once(w_ref, w_bf_ref)
    y_ref[...] = jnp.dot(xn_ref[...], w_bf_ref[...], preferred_element_type=F32)
    n_roped = GQA_HEADS + GQA_KV

    @pl.when(pl.program_id(0) < pl.num_programs(0))
    def _():
        for k in range(n_roped):
            gain = qg_ref[...] if k < GQA_HEADS else kg_ref[...]
            xg = _rms_rows(y_ref[:, k * HEAD_DIM:(k + 1) * HEAD_DIM], gain)
            r_out = xg * cb_ref[...] + pltpu.roll(xg, HEAD_DIM // 2, 1) * sb_ref[...]
            o_ref[:, k * HEAD_DIM:(k + 1) * HEAD_DIM] = r_out.astype(BF16)
        o_ref[:, n_roped * HEAD_DIM:] = y_ref[:, n_roped * HEAD_DIM:].astype(BF16)


def _proj_diff_kernel(xn_ref, w_ref, cc_ref, sca_ref, scb_ref, o_ref, w_bf_ref):
    _round_weight_once(w_ref, w_bf_ref)
    y = jnp.dot(xn_ref[...], w_bf_ref[...], preferred_element_type=F32)
    for k in range(2 * DIFF_HEADS):
        x = y[:, k * HEAD_DIM:(k + 1) * HEAD_DIM]
        r = (x * cc_ref[...] + pltpu.roll(x, HEAD_DIM - DIFF_QK_DIM // 2, 1) * sca_ref[...]
             + pltpu.roll(x, DIFF_QK_DIM // 2, 1) * scb_ref[...])
        if k < DIFF_HEADS:
            r = r * (DIFF_QK_DIM ** -0.5 * LOG2E)
        o_ref[:, k * HEAD_DIM:(k + 1) * HEAD_DIM] = r.astype(BF16)
    o_ref[:, 2 * DIFF_HEADS * HEAD_DIM:] = y[:, 2 * DIFF_HEADS * HEAD_DIM:].astype(BF16)


def _token_tile(width):
    return pl.BlockSpec((PROJ_TM, width), lambda i: (i, 0))


def _weight_group(layer, group):
    return pl.BlockSpec((None, D_MODEL, PROJ_GROUP), lambda i: (layer, 0, group), pipeline_mode=pl.Buffered(1))


def _proj_na(h, g, w, layer):
    T = h.shape[0]
    return pl.pallas_call(
        _proj_na_kernel,
        grid=(T // PROJ_TM,),
        in_specs=[_token_tile(D_MODEL), pl.BlockSpec((1, D_MODEL), lambda i: (0, 0)), _weight_group(layer, 0)],
        out_specs=[_token_tile(PROJ_GROUP), _token_tile(D_MODEL)],
        out_shape=[jax.ShapeDtypeStruct((T, PROJ_GROUP), BF16), jax.ShapeDtypeStruct((T, D_MODEL), BF16)],
        scratch_shapes=[pltpu.VMEM((D_MODEL, PROJ_GROUP), BF16)],
        compiler_params=_params("arbitrary"),
        name="proj_na",
    )(h, g.reshape(1, D_MODEL), w)


def _proj_gqa(xn, w, qg, kg, cb, sb, S, layer):
    T = xn.shape[0]
    nst = S // PROJ_TM
    table = pl.BlockSpec((PROJ_TM, HEAD_DIM), lambda i: (i % nst, 0))
    vec = pl.BlockSpec((1, HEAD_DIM), lambda i: (0, 0))
    return pl.pallas_call(
        _proj_gqa_kernel,
        grid=(T // PROJ_TM,),
        in_specs=[_token_tile(D_MODEL), _weight_group(layer, 1), vec, vec, table, table],
        out_specs=_token_tile(PROJ_GROUP),
        out_shape=jax.ShapeDtypeStruct((T, PROJ_GROUP), BF16),
        scratch_shapes=[pltpu.VMEM((D_MODEL, PROJ_GROUP), BF16), pltpu.VMEM((PROJ_TM, PROJ_GROUP), F32)],
        compiler_params=_params("arbitrary"),
        name="proj_gqa",
    )(xn, w, qg.reshape(1, HEAD_DIM), kg.reshape(1, HEAD_DIM), cb, sb)


def _proj_diff(xn, w, cc, sca, scb, S, layer):
    T = xn.shape[0]
    nst = S // PROJ_TM
    table = pl.BlockSpec((PROJ_TM, HEAD_DIM), lambda i: (i % nst, 0))
    return pl.pallas_call(
        _proj_diff_kernel,
        grid=(T // PROJ_TM,),
        in_specs=[_token_tile(D_MODEL), _weight_group(layer, 2), table, table, table],
        out_specs=_token_tile(PROJ_GROUP),
        out_shape=jax.ShapeDtypeStruct((T, PROJ_GROUP), BF16),
        scratch_shapes=[pltpu.VMEM((D_MODEL, PROJ_GROUP), BF16)],
        compiler_params=_params("arbitrary"),
        name="proj_diff",
    )(xn, w, cc, sca, scb)


def _transpose_values(v_ref, vt_ref):
    S = v_ref.shape[0]
    step = 512
    for c in range(S // step):
        blk = v_ref[c * step:(c + 1) * step, :].astype(F32)
        vt_ref[:, c * step:(c + 1) * step] = blk.T.astype(BF16)


def _attention_rows(q_ref, k_ref, vt_ref, s_refs, emit, unroll):
    npair = q_ref.shape[0] // (2 * ATT_RB)
    S = k_ref.shape[0]

    def scores(r, slot):
        off = pl.multiple_of(r * ATT_RB, ATT_RB)
        st = lax.dot_general(k_ref[...], q_ref[pl.ds(off, ATT_RB), :], (((1,), (1,)), ((), ())),
                             preferred_element_type=F32)
        s_refs[slot][...] = st
        return jnp.max(st, axis=0, keepdims=True)

    def consume(slot, m):
        acc = jnp.zeros((HEAD_DIM, ATT_RB), F32)
        l = jnp.zeros((1, ATT_RB), F32)
        for c in range(S // ATT_KC):
            p = jnp.exp2(s_refs[slot][c * ATT_KC:(c + 1) * ATT_KC, :] - m)
            l = l + jnp.sum(p, axis=0, keepdims=True)
            acc = acc + jnp.dot(vt_ref[:, c * ATT_KC:(c + 1) * ATT_KC], p.astype(BF16),
                                preferred_element_type=F32)
        return (acc / l).T

    def pairs(first_pair, m, last):
        for u in range(unroll // 2):
            pair = first_pair + u
            m_odd = scores(2 * pair + 1, 1)
            o_even = consume(0, m)
            if not (last and u == unroll // 2 - 1):
                m = scores(2 * pair + 2, 0)
            o_odd = consume(1, m_odd)
            emit(pair, o_even, o_odd)
        return m

    n_body = npair // (unroll // 2)
    m = lax.fori_loop(0, n_body - 1, lambda i, m: pairs(i * (unroll // 2), m, False), scores(0, 0))
    pairs((n_body - 1) * (unroll // 2), m, True)


def _gqa_kernel(q_ref, k_ref, v_ref, o_ref, vt_ref, s0_ref, s1_ref, qs_ref, os_ref):
    S = k_ref.shape[0]
    _transpose_values(v_ref, vt_ref)
    for g in range(GQA_GROUP):
        qs_ref[g * S:(g + 1) * S, :] = q_ref[:, g * HEAD_DIM:(g + 1) * HEAD_DIM]

    def emit(i, o_even, o_odd):
        off = pl.multiple_of(i * 2 * ATT_RB, 2 * ATT_RB)
        os_ref[pl.ds(off, ATT_RB), :] = o_even.astype(os_ref.dtype)
        os_ref[pl.ds(off + ATT_RB, ATT_RB), :] = o_odd.astype(os_ref.dtype)

    _attention_rows(qs_ref, k_ref, vt_ref, (s0_ref, s1_ref), emit, GQA_UNROLL)
    for g in range(GQA_GROUP):
        o_ref[:, g * HEAD_DIM:(g + 1) * HEAD_DIM] = os_ref[g * S:(g + 1) * S, :]


def _gqa(proj, B, S):
    T = proj.shape[0]
    gw = GQA_GROUP * HEAD_DIM
    q0 = 0
    k0 = GQA_HEADS
    v0 = k0 + GQA_KV
    return pl.pallas_call(
        _gqa_kernel,
        grid=(B, GQA_KV),
        in_specs=[
            pl.BlockSpec((S, gw), lambda b, kh: (b, q0 + kh)),
            pl.BlockSpec((S, HEAD_DIM), lambda b, kh: (b, k0 + kh)),
            pl.BlockSpec((S, HEAD_DIM), lambda b, kh: (b, v0 + kh)),
        ],
        out_specs=pl.BlockSpec((S, gw), lambda b, kh: (b, kh)),
        out_shape=jax.ShapeDtypeStruct((T, GQA_HEADS * HEAD_DIM), BF16),
        scratch_shapes=[pltpu.VMEM((HEAD_DIM, S), BF16),
                        pltpu.VMEM((S, ATT_RB), F32), pltpu.VMEM((S, ATT_RB), F32),
                        pltpu.VMEM((GQA_GROUP * S, HEAD_DIM), BF16),
                        pltpu.VMEM((GQA_GROUP * S, HEAD_DIM), BF16)],
        compiler_params=_params("parallel", "parallel", vmem_limit_bytes=VMEM_LIMIT_LARGE_BYTES),
        name="gqa",
    )(proj, proj, proj)


def _diff_kernel(lam_ref, gain_ref, q_ref, k_ref, v_ref, o_ref, vt_ref, s0_ref, s1_ref, qc_ref, *, lam_init):
    _transpose_values(v_ref, vt_ref)

    S = q_ref.shape[0]
    lane = lax.broadcasted_iota(jnp.int32, (ATT_RB, HEAD_DIM), 1)
    first = lane < DIFF_QK_DIM
    for r in range(S // ATT_RB):
        q = q_ref[r * ATT_RB:(r + 1) * ATT_RB, :]
        zero = jnp.zeros_like(q)
        qc_ref[2 * r * ATT_RB:(2 * r + 1) * ATT_RB, :] = jnp.where(first, q, zero)
        qc_ref[(2 * r + 1) * ATT_RB:(2 * r + 2) * ATT_RB, :] = jnp.where(first, zero, q)

    lp = lam_ref[...]
    lam = (jnp.exp(jnp.sum(lp[0:1] * lp[1:2], axis=-1, keepdims=True))
           - jnp.exp(jnp.sum(lp[2:3] * lp[3:4], axis=-1, keepdims=True)) + lam_init)

    def emit(i, o1, o2):
        off = pl.multiple_of(i * ATT_RB, ATT_RB)
        o = o1 - lam * o2
        o = _rms_rows(o, gain_ref[...]) * (1.0 - lam_init)
        o_ref[pl.ds(off, ATT_RB), :] = o.astype(o_ref.dtype)

    _attention_rows(qc_ref, k_ref, vt_ref, (s0_ref, s1_ref), emit, DIFF_UNROLL)


def _diff(proj, lam_params, gain, layer, B, S):
    T = proj.shape[0]
    q0 = 0
    k0 = DIFF_HEADS
    v0 = 2 * DIFF_HEADS
    lam_init = 0.8 - 0.6 * math.exp(-0.3 * layer)
    return pl.pallas_call(
        functools.partial(_diff_kernel, lam_init=lam_init),
        grid=(B, DIFF_HEADS),
        in_specs=[
            pl.BlockSpec((4, DIFF_QK_DIM), lambda b, h: (0, 0)),
            pl.BlockSpec((1, HEAD_DIM), lambda b, h: (0, 0)),
            pl.BlockSpec((S, HEAD_DIM), lambda b, h: (b, q0 + h)),
            pl.BlockSpec((S, HEAD_DIM), lambda b, h: (b, k0 + h)),
            pl.BlockSpec((S, HEAD_DIM), lambda b, h: (b, v0 + h)),
        ],
        out_specs=pl.BlockSpec((S, HEAD_DIM), lambda b, h: (b, h)),
        out_shape=jax.ShapeDtypeStruct((T, DIFF_HEADS * HEAD_DIM), BF16),
        scratch_shapes=[pltpu.VMEM((HEAD_DIM, S), BF16),
                        pltpu.VMEM((S, ATT_RB), F32), pltpu.VMEM((S, ATT_RB), F32),
                        pltpu.VMEM((2 * S, HEAD_DIM), BF16)],
        compiler_params=_params("parallel", "parallel", vmem_limit_bytes=VMEM_LIMIT_LARGE_BYTES),
        name="diff",
    )(lam_params, gain.reshape(1, HEAD_DIM), proj, proj, proj)


def _na_window(blk, rows):
    r0 = blk * NA_QROWS
    return r0, int(np.clip(r0 - NA_KH // 2, 0, rows - NA_KROWS))


def _na_slab_index(r0, ws, k, j, rows):
    n_dr = 2 * NA_KH - 1
    kr = ws + k
    a, valid = [], []
    for r in (r0 + 2 * j, r0 + 2 * j + 1):
        rs = int(np.clip(r - NA_KH // 2, 0, rows - NA_KH))
        valid.append(rs <= kr < rs + NA_KH)
        a.append(kr - r + (NA_KH - 1))
    if valid[0] and valid[1]:
        return a[0] - 1
    if valid[0]:
        return (n_dr - 1) + a[0]
    if valid[1]:
        return (n_dr - 1) + n_dr + a[1]
    return None


def _na_kernel(q_ref, k_ref, v_ref, slab_ref, o_ref, vt_ref, *, rows):
    _transpose_values(v_ref, vt_ref)
    tq = NA_QROWS * GRID_W
    tk = NA_KROWS * GRID_W
    masked = jnp.full((GRID_W, 2 * GRID_W), NEG_BIG, F32)
    for blk in range(rows // NA_QROWS):
        r0, ws = _na_window(blk, rows)
        bias_rows = []
        for k in range(NA_KROWS):
            slabs = []
            for j in range(NA_QROWS // 2):
                idx = _na_slab_index(r0, ws, k, j, rows)
                slabs.append(masked if idx is None else slab_ref[idx])
            bias_rows.append(jnp.concatenate(slabs, axis=1))
        bias = jnp.concatenate(bias_rows, axis=0)
        q = q_ref[r0 * GRID_W:r0 * GRID_W + tq, :]
        kw = k_ref[ws * GRID_W:ws * GRID_W + tk, :]
        st = lax.dot_general(kw, q, (((1,), (1,)), ((), ())), preferred_element_type=F32) + bias
        m = jnp.max(st, axis=0, keepdims=True)
        p = jnp.exp2(st - m)
        l = jnp.sum(p, axis=0, keepdims=True)
        ot = jnp.dot(vt_ref[:, ws * GRID_W:ws * GRID_W + tk], p.astype(BF16), preferred_element_type=F32)
        o_ref[r0 * GRID_W:r0 * GRID_W + tq, :] = (ot / l).T.astype(o_ref.dtype)


def _na(proj, slabs, B, S, layer):
    T = proj.shape[0]
    rows = S // GRID_W
    n_slab = slabs.shape[2]
    return pl.pallas_call(
        functools.partial(_na_kernel, rows=rows),
        grid=(B, NA_HEADS),
        in_specs=[
            pl.BlockSpec((S, HEAD_DIM), lambda b, h: (b, h)),
            pl.BlockSpec((S, HEAD_DIM), lambda b, h: (b, NA_HEADS + h)),
            pl.BlockSpec((S, HEAD_DIM), lambda b, h: (b, 2 * NA_HEADS + h)),
            pl.BlockSpec((None, None, n_slab, GRID_W, 2 * GRID_W), lambda b, h: (layer, h, 0, 0, 0)),
        ],
        out_specs=pl.BlockSpec((S, HEAD_DIM), lambda b, h: (b, h)),
        out_shape=jax.ShapeDtypeStruct((T, NA_HEADS * HEAD_DIM), BF16),
        scratch_shapes=[pltpu.VMEM((HEAD_DIM, S), BF16)],
        compiler_params=_params("parallel", "parallel"),
        name="na",
    )(proj, proj, proj, slabs)


def _na_bias_slabs(rpb):
    n_dc = 2 * NA_KW - 1
    cq = np.arange(GRID_W)
    cs = np.clip(cq - NA_KW // 2, 0, GRID_W - NA_KW)
    ck = np.arange(GRID_W)
    valid_c = (ck[:, None] >= cs[None, :]) & (ck[:, None] < cs[None, :] + NA_KW)
    dc = ck[:, None] - cq[None, :] + (NA_KW - 1)
    onehot = ((dc[None] == np.arange(n_dc)[:, None, None]) & valid_c[None]).astype(np.float32)
    t = jnp.einsum('lhad,dkq->lhakq', rpb, jnp.asarray(onehot), precision=lax.Precision.HIGHEST)
    t = (t + jnp.asarray(np.where(valid_c, 0.0, NEG_BIG).astype(np.float32))) * LOG2E
    masked = jnp.full_like(t, NEG_BIG)
    both = jnp.concatenate([t[:, :, 1:], t[:, :, :-1]], axis=-1)
    left = jnp.concatenate([t, masked], axis=-1)
    right = jnp.concatenate([masked, t], axis=-1)
    return jnp.concatenate([both, left, right], axis=2)


def _outproj_kernel(h_ref, a_ref, b_ref, c_ref, w_ref, o_ref, w_bf_ref):
    @pl.when(pl.program_id(0) == 0)
    def _():
        w_bf_ref[...] = w_ref[...].astype(BF16)

    na = a_ref.shape[1]
    nb = b_ref.shape[1]
    acc = jnp.dot(a_ref[...], w_bf_ref[0:na, :], preferred_element_type=F32)
    acc += jnp.dot(b_ref[...], w_bf_ref[na:na + nb, :], preferred_element_type=F32)
    acc += jnp.dot(c_ref[...], w_bf_ref[na + nb:, :], preferred_element_type=F32)
    o_ref[...] = h_ref[...] + acc


def _outproj(h, oa, ob, oc, w, layer):
    T = h.shape[0]
    return pl.pallas_call(
        _outproj_kernel,
        grid=(T // OUT_TM,),
        in_specs=[
            pl.BlockSpec((OUT_TM, D_MODEL), lambda i: (i, 0)),
            pl.BlockSpec((OUT_TM, oa.shape[1]), lambda i: (i, 0)),
            pl.BlockSpec((OUT_TM, ob.shape[1]), lambda i: (i, 0)),
            pl.BlockSpec((OUT_TM, oc.shape[1]), lambda i: (i, 0)),
            pl.BlockSpec((None, D_MODEL, D_MODEL), lambda i: (layer, 0, 0), pipeline_mode=pl.Buffered(1)),
        ],
        out_specs=pl.BlockSpec((OUT_TM, D_MODEL), lambda i: (i, 0)),
        out_shape=jax.ShapeDtypeStruct((T, D_MODEL), F32),
        scratch_shapes=[pltpu.VMEM((D_MODEL, D_MODEL), BF16)],
        compiler_params=_params("arbitrary"),
        name="outproj",
    )(h, oa, ob, oc, w)


def _ple_kernel(h_ref, g_ref, wg_ref, p_ref, wp_ref, gf_ref, o_ref, wg_bf_ref, wp_bf_ref, *, final):
    @pl.when(pl.program_id(0) == 0)
    def _():
        wg_bf_ref[...] = wg_ref[...].astype(BF16)
        wp_bf_ref[...] = wp_ref[...].astype(BF16)

    x = h_ref[...]
    xn = _rms_rows(x, g_ref[...]).astype(BF16)
    gate = jax.nn.sigmoid(jnp.dot(xn, wg_bf_ref[...], preferred_element_type=F32))
    emb = jnp.dot(p_ref[...].astype(BF16), wp_bf_ref[...], preferred_element_type=F32)
    y = x + gate * emb
    if final:
        y = _rms_rows(y, gf_ref[...])
    o_ref[...] = y


def _ple(h, g, wg, p, wp, g_final, layer, final):
    T = h.shape[0]
    resident = pl.Buffered(1)
    return pl.pallas_call(
        functools.partial(_ple_kernel, final=final),
        grid=(T // PLE_TM,),
        in_specs=[
            pl.BlockSpec((PLE_TM, D_MODEL), lambda i: (i, 0)),
            pl.BlockSpec((1, D_MODEL), lambda i: (0, 0)),
            pl.BlockSpec((None, D_MODEL, D_MODEL), lambda i: (layer, 0, 0), pipeline_mode=resident),
            pl.BlockSpec((None, PLE_TM, PLE_DIM), lambda i: (layer, i, 0)),
            pl.BlockSpec((None, PLE_DIM, D_MODEL), lambda i: (layer, 0, 0), pipeline_mode=resident),
            pl.BlockSpec((1, D_MODEL), lambda i: (0, 0)),
        ],
        out_specs=pl.BlockSpec((PLE_TM, D_MODEL), lambda i: (i, 0)),
        out_shape=jax.ShapeDtypeStruct((T, D_MODEL), F32),
        scratch_shapes=[pltpu.VMEM((D_MODEL, D_MODEL), BF16), pltpu.VMEM((PLE_DIM, D_MODEL), BF16)],
        compiler_params=_params("arbitrary"),
        name="ple",
    )(h, g.reshape(1, D_MODEL), wg, p, wp, g_final.reshape(1, D_MODEL))


def _rope_tables(S):
    t = jnp.arange(S)
    inv_ax = jnp.power(ROPE_THETA, -jnp.arange(0, AXIAL_DIM, 2, dtype=F32) / AXIAL_DIM)
    ang_b = jnp.concatenate([(t // GRID_W).astype(F32)[:, None] * inv_ax,
                             (t % GRID_W).astype(F32)[:, None] * inv_ax], axis=-1)
    cos_b, sin_b = jnp.cos(ang_b), jnp.sin(ang_b)
    cb = jnp.concatenate([cos_b, cos_b], axis=-1)
    sb = jnp.concatenate([-sin_b, sin_b], axis=-1)
    inv_1d = jnp.power(ROPE_THETA, -jnp.arange(0, DIFF_QK_DIM, 2, dtype=F32) / DIFF_QK_DIM)
    ang_c = t.astype(F32)[:, None] * inv_1d
    cos_c, sin_c = jnp.cos(ang_c), jnp.sin(ang_c)
    zero = jnp.zeros_like(sin_c)
    cc = jnp.concatenate([cos_c] * 4, axis=-1)
    sca = jnp.concatenate([-sin_c, zero, -sin_c, zero], axis=-1)
    scb = jnp.concatenate([zero, sin_c, zero, sin_c], axis=-1)
    return cb, sb, cc, sca, scb


def kernel(x, p, g_ffn1, w1_gate, w1_up, w1_down, g_mix, w_in, na_rpb, gqa_q_gain, gqa_k_gain,
           diff_lambda, diff_subln_gain, w_out, g_ffn2, w2_gate, w2_up, w2_down, g_ple,
           w_ple_gate, w_ple_proj, g_final):
    B, S, D = x.shape
    depth = w_in.shape[0]
    T = B * S
    rows = S // GRID_W
    assert D == D_MODEL and w_in.shape[1:] == (D_MODEL, IN_COLS) and w1_gate.shape[1:] == (D_MODEL, D_FF)
    assert p.shape == (depth, B, S, PLE_DIM) and na_rpb.shape[1:] == (NA_HEADS, 2 * NA_KH - 1, 2 * NA_KW - 1)
    assert S % GRID_W == 0 and rows % NA_QROWS == 0 and rows >= NA_KROWS
    assert S % (ATT_RB * max(GQA_UNROLL, DIFF_UNROLL)) == 0 and S % ATT_KC == 0
    assert T % FFN_TM == 0 and S % PROJ_TM == 0 and T % OUT_TM == 0 and T % PLE_TM == 0
    cb, sb, cc, sca, scb = _rope_tables(S)
    slabs = _na_bias_slabs(na_rpb)
    p_flat = p.reshape(depth, T, PLE_DIM)
    h = x.reshape(T, D)
    for i in range(depth):
        h = _ffn(h, g_ffn1[i], w1_gate, w1_up, w1_down, i)
        proj_a, xn = _proj_na(h, g_mix[i], w_in, i)
        q_gain = gqa_q_gain[i] * (HEAD_DIM ** -0.5 * LOG2E)
        proj_b = _proj_gqa(xn, w_in, q_gain, gqa_k_gain[i], cb, sb, S, i)
        proj_c = _proj_diff(xn, w_in, cc, sca, scb, S, i)
        out_a = _na(proj_a, slabs, B, S, i)
        out_b = _gqa(proj_b, B, S)
        out_c = _diff(proj_c, diff_lambda[i], diff_subln_gain[i], i, B, S)
        h = _outproj(h, out_a, out_b, out_c, w_out, i)
        h = _ffn(h, g_ffn2[i], w2_gate, w2_up, w2_down, i)
        h = _ple(h, g_ple[i], w_ple_gate, p_flat, w_ple_proj, g_final, i, final=(i == depth - 1))
    return h.reshape(B, S, D)
```
